```python
import math
import jax, jax.numpy as jnp
from jax import lax
import numpy as np


D_MODEL = 4096
BATCH = 2
SEQ = 8192
DEPTH = 2
DEC_BATCH = 1
DEC_SEQ = 16384
PAST_LEN = 128

ROPE_THETA = 10000.0
LN_EPS = 1e-5
RMS_EPS = 1e-6
NEG_INF = -1e30
DEEPNORM_ALPHA = (2 * DEPTH) ** 0.25
DEEPNORM_BETA = (8 * DEPTH) ** -0.25

N_BRANCH = 3
BRANCH_WIDTH = D_MODEL // 2

GLA_HEADS = 4
GLA_KEY_DIM = BRANCH_WIDTH // 2
GLA_VAL_DIM = BRANCH_WIDTH
GLA_HEAD_K = GLA_KEY_DIM // GLA_HEADS
GLA_HEAD_V = GLA_VAL_DIM // GLA_HEADS
GLA_GATE_RANK = 16
GLA_GATE_NORMALIZER = 16.0
GLA_CHUNK = 64

DIL_PATTERNS = ((128, 1), (512, 4), (2048, 16))
DIL_N_GROUPS = len(DIL_PATTERNS)
DIL_HEAD_DIM = 128
DIL_SLOT_HEADS = BRANCH_WIDTH // DIL_HEAD_DIM

MLA_HEADS = BRANCH_WIDTH // 128
MLA_NOPE_DIM = 128
MLA_ROPE_DIM = 64
MLA_V_DIM = 128
MLA_Q_RANK = 896
MLA_KV_RANK = 256
MLA_Q_BLOCK = 128

D_FF = 11008

IN_SIZES = (
    GLA_KEY_DIM, GLA_KEY_DIM, GLA_VAL_DIM, GLA_VAL_DIM, GLA_GATE_RANK, GLA_GATE_RANK,
    DIL_N_GROUPS * 3 * BRANCH_WIDTH,
    MLA_Q_RANK, MLA_KV_RANK, MLA_ROPE_DIM,
    N_BRANCH * D_MODEL,
)

kernel_name = 'hybrid_gla_dilated_mla_macaron_encoder'


def _in_split_points():
    pts, acc = [], 0
    for s in IN_SIZES[:-1]:
        acc += s
        pts.append(acc)
    return pts


def layer_norm(x, g, b):
    xf = x.astype(jnp.float32)
    mu = jnp.mean(xf, axis=-1, keepdims=True)
    var = jnp.mean(jnp.square(xf - mu), axis=-1, keepdims=True)
    return ((xf - mu) * lax.rsqrt(var + LN_EPS) * g + b).astype(x.dtype)


def rms_norm(x, g):
    xf = x.astype(jnp.float32)
    return (xf * lax.rsqrt(jnp.mean(jnp.square(xf), axis=-1, keepdims=True) + RMS_EPS) * g).astype(x.dtype)


def apply_rope(x, positions):
    dim = x.shape[-1]
    half = dim // 2
    inv = jnp.power(ROPE_THETA, -2.0 * jnp.arange(half, dtype=jnp.float32) / dim)
    ang = positions.astype(jnp.float32)[:, None] * inv[None, :]
    cos = jnp.cos(ang)[:, None, :]
    sin = jnp.sin(ang)[:, None, :]
    xf = x.astype(jnp.float32)
    x1, x2 = xf[..., :half], xf[..., half:]
    return jnp.concatenate([x1 * cos - x2 * sin, x2 * cos + x1 * sin], axis=-1).astype(x.dtype)


def swiglu(x, w_gate, w_up, w_down):
    return (jax.nn.silu(x @ w_gate) * (x @ w_up)) @ w_down


def gla_chunk_scan(q, k, v, log_g, strict):
    Bn, H, S, dk = q.shape
    dv = v.shape[-1]
    C = GLA_CHUNK
    nC = S // C

    def chunks(t):
        return jnp.moveaxis(t.reshape(Bn, H, nC, C, t.shape[-1]), 2, 0)

    mask = jnp.tril(jnp.ones((C, C), dtype=bool), k=-1 if strict else 0)

    def step(state, inp):
        qc, kc, vc, gc = inp
        b = jnp.cumsum(gc, axis=-2)
        diff = b[..., :, None, :] - b[..., None, :, :]
        decay = jnp.exp(jnp.where(mask[:, :, None], diff, -jnp.inf))
        attn = jnp.einsum('bhid,bhjd,bhijd->bhij', qc, kc, decay)
        o_intra = jnp.einsum('bhij,bhje->bhie', attn, vc)
        o_inter = jnp.einsum('bhid,bhde->bhie', qc * jnp.exp(b), state)
        b_last = b[..., -1:, :]
        new_state = (jnp.swapaxes(jnp.exp(b_last), -1, -2) * state
                     + jnp.einsum('bhjd,bhje->bhde', kc * jnp.exp(b_last - b), vc))
        return new_state, o_intra + o_inter

    init = jnp.zeros((Bn, H, dk, dv), jnp.float32)
    _, outs = lax.scan(step, init, (chunks(q), chunks(k), chunks(v), chunks(log_g)))
    return jnp.moveaxis(outs, 0, 2).reshape(Bn, H, S, dv)


def gla_branch(a_q, a_k, a_v, a_r, a_gf, a_gb, gate_w, gate_b, norm_g):
    Bn, S, _ = a_q.shape

    def heads(t, hd):
        return jnp.transpose(t.astype(jnp.float32).reshape(Bn, S, GLA_HEADS, hd), (0, 2, 1, 3))

    qh = heads(a_q, GLA_HEAD_K) * (GLA_HEAD_K ** -0.5)
    kh = heads(a_k, GLA_HEAD_K)
    vh = heads(a_v, GLA_HEAD_V)
    lg_f = heads(jax.nn.log_sigmoid((a_gf @ gate_w[0] + gate_b[0]).astype(jnp.float32)) / GLA_GATE_NORMALIZER, GLA_HEAD_K)
    lg_b = heads(jax.nn.log_sigmoid((a_gb @ gate_w[1] + gate_b[1]).astype(jnp.float32)) / GLA_GATE_NORMALIZER, GLA_HEAD_K)
    o_f = gla_chunk_scan(qh, kh, vh, lg_f, strict=False)

    def flip(t):
        return jnp.flip(t, axis=2)

    o_b = flip(gla_chunk_scan(flip(qh), flip(kh), flip(vh), flip(lg_b), strict=True))
    o = jnp.transpose(o_f + o_b, (0, 2, 1, 3))
    r = a_r.astype(jnp.float32).reshape(Bn, S, GLA_HEADS, GLA_HEAD_V)
    o = rms_norm(o, norm_g) * jax.nn.silu(r)
    return o.reshape(Bn, S, GLA_VAL_DIM).astype(a_q.dtype)


def banded_attention(q, k, v, radius):
    lead = q.shape[:-2]
    L, hd = q.shape[-2], q.shape[-1]
    R = radius
    nb = -(-L // R)
    Lp = nb * R

    def pad(t, lo, hi):
        return jnp.pad(t.astype(jnp.float32), [(0, 0)] * len(lead) + [(lo, hi), (0, 0)])

    qb = pad(q, 0, Lp - L).reshape(*lead, nb, R, hd)

    def windows(t):
        tb = pad(t, R, Lp - L + R).reshape(*lead, nb + 2, R, hd)
        return jnp.concatenate([tb[..., :-2, :, :], tb[..., 1:-1, :, :], tb[..., 2:, :, :]], axis=-2)

    kw, vw = windows(k), windows(v)
    q_pos = jnp.arange(Lp).reshape(nb, R)
    k_pos = jnp.arange(nb)[:, None] * R - R + jnp.arange(3 * R)[None, :]
    kp = k_pos[:, None, :]
    valid = (jnp.abs(q_pos[:, :, None] - kp) <= R) & (kp >= 0) & (kp < L)
    s = jnp.einsum('...nqd,...nkd->...nqk', qb, kw) * (hd ** -0.5)
    s = jnp.where(valid, s, NEG_INF)
    m = jnp.max(s, axis=-1, keepdims=True)
    p = jnp.exp(s - m)
    den = jnp.sum(p, axis=-1, keepdims=True)
    o = jnp.einsum('...nqk,...nkd->...nqd', p, vw) / den
    lse = (m + jnp.log(den))[..., 0]
    o = o.reshape(*lead, Lp, hd)[..., :L, :]
    lse = lse.reshape(*lead, Lp)[..., :L]
    return o, lse


def dilated_group(q, k, v, window, dilation):
    Bn, S, H, hd = q.shape
    L = S // dilation

    def to_sub(t):
        return jnp.transpose(t.reshape(Bn, L, dilation, H, hd), (0, 2, 3, 1, 4))

    o, lse = banded_attention(to_sub(q), to_sub(k), to_sub(v), window // (2 * dilation))
    o = jnp.transpose(o, (0, 3, 1, 2, 4)).reshape(Bn, S, H, hd)
    lse = jnp.transpose(lse, (0, 3, 1, 2)).reshape(Bn, S, H)
    return o, lse


def dilated_branch(b_qkv, positions):
    Bn, S, _ = b_qkv.shape
    qkv = b_qkv.reshape(Bn, S, DIL_N_GROUPS, 3, DIL_SLOT_HEADS, DIL_HEAD_DIM)
    outs, lses = [], []
    for g, (window, dilation) in enumerate(DIL_PATTERNS):
        q = apply_rope(qkv[:, :, g, 0], positions)
        k = apply_rope(qkv[:, :, g, 1], positions)
        o, lse = dilated_group(q, k, qkv[:, :, g, 2], window, dilation)
        outs.append(o)
        lses.append(lse)
    w = jax.nn.softmax(jnp.stack(lses, axis=0), axis=0)
    o = jnp.sum(w[..., None] * jnp.stack(outs, axis=0), axis=0)
    return o.reshape(Bn, S, BRANCH_WIDTH).astype(b_qkv.dtype)


def mla_branch(c_q, c_kv, k_rope_raw, q_norm_g, w_uq, kv_norm_g, w_ukv, positions):
    Bn, S, _ = c_q.shape
    q = (rms_norm(c_q, q_norm_g) @ w_uq).reshape(Bn, S, MLA_HEADS, MLA_NOPE_DIM + MLA_ROPE_DIM)
    q_nope = q[..., :MLA_NOPE_DIM]
    q_rope = apply_rope(q[..., MLA_NOPE_DIM:], positions)
    kv = (rms_norm(c_kv, kv_norm_g) @ w_ukv).reshape(Bn, S, MLA_HEADS, MLA_NOPE_DIM + MLA_V_DIM)
    k_nope = kv[..., :MLA_NOPE_DIM]
    v = kv[..., MLA_NOPE_DIM:].astype(jnp.float32)
    k_rope = apply_rope(k_rope_raw[:, :, None, :], positions)[:, :, 0]
    scale = (MLA_NOPE_DIM + MLA_ROPE_DIM) ** -0.5
    nqb = S // MLA_Q_BLOCK

    def blocks(t):
        return jnp.moveaxis(t.reshape(Bn, nqb, MLA_Q_BLOCK, *t.shape[2:]), 1, 0)

    def attend(qs):
        qn, qr = qs
        s = (jnp.einsum('bqhd,bkhd->bhqk', qn, k_nope, preferred_element_type=jnp.float32)
             + jnp.einsum('bqhd,bkd->bhqk', qr, k_rope, preferred_element_type=jnp.float32)) * scale
        p = jax.nn.softmax(s, axis=-1)
        return jnp.einsum('bhqk,bkhd->bqhd', p, v)

    o = lax.map(attend, (blocks(q_nope), blocks(q_rope)))
    o = jnp.moveaxis(o, 0, 1).reshape(Bn, S, MLA_HEADS * MLA_V_DIM)
    return o.astype(c_q.dtype)


def encoder_layer(x, w_in, gla_gate_w, gla_gate_b, gla_norm_g, mla_q_norm_g, mla_uq, mla_kv_norm_g, mla_ukv,
                  w_branch, w_out, ffn_w_gate, ffn_w_up, ffn_w_down, ln_g, ln_b):
    Bn, S, _ = x.shape
    positions = jnp.arange(S, dtype=jnp.int32)
    x = layer_norm(DEEPNORM_ALPHA * x + 0.5 * swiglu(x, ffn_w_gate[0], ffn_w_up[0], ffn_w_down[0]), ln_g[0], ln_b[0])
    h = x @ w_in
    (a_q, a_k, a_v, a_r, a_gf, a_gb, b_qkv, c_q, c_kv, c_kr, gate_logits) = jnp.split(h, _in_split_points(), axis=-1)
    y_a = gla_branch(a_q, a_k, a_v, a_r, a_gf, a_gb, gla_gate_w, gla_gate_b, gla_norm_g)
    y_b = dilated_branch(b_qkv, positions)
    y_c = mla_branch(c_q, c_kv, c_kr, mla_q_norm_g, mla_uq, mla_kv_norm_g, mla_ukv, positions)
    gates = jax.nn.sigmoid(gate_logits.astype(jnp.float32)).reshape(Bn, S, N_BRANCH, D_MODEL).astype(x.dtype)
    merged = (gates[:, :, 0] * (y_a @ w_branch[0])
              + gates[:, :, 1] * (y_b @ w_branch[1])
              + gates[:, :, 2] * (y_c @ w_branch[2]))
    x = layer_norm(DEEPNORM_ALPHA * x + merged @ w_out, ln_g[1], ln_b[1])
    x = layer_norm(DEEPNORM_ALPHA * x + 0.5 * swiglu(x, ffn_w_gate[1], ffn_w_up[1], ffn_w_down[1]), ln_g[2], ln_b[2])
    return x


def setup_inputs(seed: int = 0) -> dict:
    key = jax.random.key(seed)
    ks = jax.random.split(key, 22)
    beta = DEEPNORM_BETA
    std_in = D_MODEL ** -0.5

    def nrm(k, shape, scale):
        return jax.random.normal(k, shape, jnp.float32) * scale

    x_prompt = nrm(ks[0], (BATCH, SEQ, D_MODEL), 1.0)
    x_sample = nrm(ks[1], (DEC_BATCH, DEC_SEQ, D_MODEL), 1.0)
    gla_qk = nrm(ks[2], (DEPTH, D_MODEL, 2 * GLA_KEY_DIM), std_in)
    gla_v = nrm(ks[3], (DEPTH, D_MODEL, GLA_VAL_DIM), std_in * beta)
    gla_r_g = nrm(ks[4], (DEPTH, D_MODEL, GLA_VAL_DIM + 2 * GLA_GATE_RANK), std_in)
    qkv_scale = jnp.array([1.0, 1.0, beta], jnp.float32).reshape(1, 1, 1, 3, 1)
    dil = (nrm(ks[5], (DEPTH, D_MODEL, DIL_N_GROUPS, 3, BRANCH_WIDTH), std_in) * qkv_scale).reshape(
        DEPTH, D_MODEL, DIL_N_GROUPS * 3 * BRANCH_WIDTH)
    mla_cols = nrm(ks[6], (DEPTH, D_MODEL, MLA_Q_RANK + MLA_KV_RANK + MLA_ROPE_DIM), std_in)
    gate_cols = nrm(ks[7], (DEPTH, D_MODEL, N_BRANCH * D_MODEL), std_in)
    w_in = jnp.concatenate([gla_qk, gla_v, gla_r_g, dil, mla_cols, gate_cols], axis=-1)
    gla_gate_w = nrm(ks[8], (DEPTH, 2, GLA_GATE_RANK, GLA_KEY_DIM), GLA_GATE_RANK ** -0.5)
    gla_gate_b = nrm(ks[9], (DEPTH, 2, GLA_KEY_DIM), 0.1)
    gla_norm_g = 1.0 + nrm(ks[10], (DEPTH, GLA_HEAD_V), 0.01)
    mla_q_norm_g = 1.0 + nrm(ks[11], (DEPTH, MLA_Q_RANK), 0.01)
    mla_uq = nrm(ks[12], (DEPTH, MLA_Q_RANK, MLA_HEADS * (MLA_NOPE_DIM + MLA_ROPE_DIM)), MLA_Q_RANK ** -0.5)
    mla_kv_norm_g = 1.0 + nrm(ks[13], (DEPTH, MLA_KV_RANK), 0.01)
    kv_scale = jnp.concatenate([jnp.ones((MLA_NOPE_DIM,), jnp.float32), jnp.full((MLA_V_DIM,), beta, jnp.float32)])
    mla_ukv = (nrm(ks[14], (DEPTH, MLA_KV_RANK, MLA_HEADS, MLA_NOPE_DIM + MLA_V_DIM), MLA_KV_RANK ** -0.5)
               * kv_scale).reshape(DEPTH, MLA_KV_RANK, MLA_HEADS * (MLA_NOPE_DIM + MLA_V_DIM))
    w_branch = nrm(ks[15], (DEPTH, N_BRANCH, BRANCH_WIDTH, D_MODEL), BRANCH_WIDTH ** -0.5 * beta)
    w_out = nrm(ks[16], (DEPTH, D_MODEL, D_MODEL), std_in * beta)
    ffn_w_gate = nrm(ks[17], (DEPTH, 2, D_MODEL, D_FF), std_in * beta)
    ffn_w_up = nrm(ks[18], (DEPTH, 2, D_MODEL, D_FF), std_in * beta)
    ffn_w_down = nrm(ks[19], (DEPTH, 2, D_FF, D_MODEL), D_FF ** -0.5 * beta)
    ln_g = 1.0 + nrm(ks[20], (DEPTH, 3, D_MODEL), 0.01)
    ln_b = nrm(ks[21], (DEPTH, 3, D_MODEL), 0.01)
    return {'x_prompt': x_prompt, 'x_sample': x_sample, 'w_in': w_in, 'gla_gate_w': gla_gate_w,
            'gla_gate_b': gla_gate_b, 'gla_norm_g': gla_norm_g, 'mla_q_norm_g': mla_q_norm_g, 'mla_uq': mla_uq,
            'mla_kv_norm_g': mla_kv_norm_g, 'mla_ukv': mla_ukv, 'w_branch': w_branch, 'w_out': w_out,
            'ffn_w_gate': ffn_w_gate, 'ffn_w_up': ffn_w_up, 'ffn_w_down': ffn_w_down, 'ln_g': ln_g, 'ln_b': ln_b}


def reference(x_prompt, x_sample, w_in, gla_gate_w, gla_gate_b, gla_norm_g, mla_q_norm_g, mla_uq, mla_kv_norm_g,
              mla_ukv, w_branch, w_out, ffn_w_gate, ffn_w_up, ffn_w_down, ln_g, ln_b):
    def trunk(x):
        for l in range(DEPTH):
            x = encoder_layer(x, w_in[l], gla_gate_w[l], gla_gate_b[l], gla_norm_g[l], mla_q_norm_g[l], mla_uq[l],
                              mla_kv_norm_g[l], mla_ukv[l], w_branch[l], w_out[l], ffn_w_gate[l], ffn_w_up[l],
                              ffn_w_down[l], ln_g[l], ln_b[l])
        return x

    y_prompt = trunk(x_prompt)
    y_sample = trunk(x_sample)
    return (y_prompt, y_sample)
```

```python
import functools
import math

import jax
import jax.numpy as jnp
from jax import lax
from jax.experimental import pallas as pl
from jax.experimental.pallas import tpu as pltpu

F32 = jnp.float32
BF16 = jnp.bfloat16

D_MODEL = 4096
N_LAYERS = 2
D_FF = 11008
D_FF_PAD = 11264
ROPE_THETA = 10000.0
LN_EPS = 1e-5
RMS_EPS = 1e-6
NEG_INF = -1e30
DEEPNORM_ALPHA = (2 * N_LAYERS) ** 0.25
BRANCH_W = 2048

GLA_HEADS = 4
GLA_DK = 256
GLA_DV = 512
GLA_RANK = 16
GLA_NORMALIZER = 16.0
GLA_CHUNK = 64
GLA_SUB = 16
GLA_BLOCK = 256

DIL_DILATIONS = (1, 4, 16)
DIL_RADIUS = 64
DIL_HEADS = 16
DIL_HD = 128
DIL_TILE = 256

MLA_HEADS = 16
MLA_NOPE = 128
MLA_ROPE = 64
MLA_V = 128
MLA_QRANK = 896
MLA_KVRANK = 256
MLA_HPAD = 256
C_COLS = 1408
C_GATE_BLOCK = 10

LANE = 128
VMEM_LIMIT = 56 * 1024 * 1024


def _params(sem, vmem=VMEM_LIMIT):
    return pltpu.CompilerParams(dimension_semantics=sem, vmem_limit_bytes=vmem)


def _sigmoid(x):
    return 1.0 / (1.0 + jnp.exp(-x))


def _proj_kernel(x_ref, w_ref, o_ref):
    o_ref[...] = jnp.dot(x_ref[...], w_ref[...], preferred_element_type=F32).astype(o_ref.dtype)


def _proj(x, w, out_dtype, tm, tn, name):
    m, k = x.shape
    n = w.shape[1]
    return pl.pallas_call(
        _proj_kernel,
        out_shape=jax.ShapeDtypeStruct((m, n), out_dtype),
        grid=(m // tm, n // tn),
        in_specs=[pl.BlockSpec((tm, k), lambda i, j: (i, 0)),
                  pl.BlockSpec((k, tn), lambda i, j: (0, j))],
        out_specs=pl.BlockSpec((tm, tn), lambda i, j: (i, j)),
        compiler_params=_params(("parallel", "arbitrary")),
        name=name,
    )(x, w)


def _proj_rope_kernel(x_ref, w_ref, cos_ref, sin_ref, o_ref, *, n_q_tiles, q_scale):
    acc = jnp.dot(x_ref[...], w_ref[...], preferred_element_type=F32)
    scale = jnp.where(pl.program_id(1) < n_q_tiles, q_scale, 1.0).astype(F32)
    cos = cos_ref[...] * scale
    sin = sin_ref[...] * scale
    for h in range(acc.shape[1] // LANE):
        y = acc[:, h * LANE:(h + 1) * LANE]
        o_ref[:, h * LANE:(h + 1) * LANE] = (y * cos + pltpu.roll(y, LANE // 2, 1) * sin).astype(o_ref.dtype)


def _proj_rope(x, w, cos, sin, seq, n_q_cols, q_scale, tm, tn, name):
    m, k = x.shape
    n = w.shape[1]
    tiles_per_seq = seq // tm
    return pl.pallas_call(
        functools.partial(_proj_rope_kernel, n_q_tiles=n_q_cols // tn, q_scale=q_scale),
        out_shape=jax.ShapeDtypeStruct((m, n), BF16),
        grid=(m // tm, n // tn),
        in_specs=[pl.BlockSpec((tm, k), lambda i, j: (i, 0)),
                  pl.BlockSpec((k, tn), lambda i, j: (0, j)),
                  pl.BlockSpec((tm, LANE), lambda i, j: (i % tiles_per_seq, 0)),
                  pl.BlockSpec((tm, LANE), lambda i, j: (i % tiles_per_seq, 0))],
        out_specs=pl.BlockSpec((tm, tn), lambda i, j: (i, j)),
        compiler_params=_params(("parallel", "arbitrary")),
        name=name,
    )(x, w, cos, sin)


def _ffn_up_kernel(x_ref, wg_ref, wu_ref, o_ref):
    x = x_ref[...]
    g = jnp.dot(x, wg_ref[...], preferred_element_type=F32)
    u = jnp.dot(x, wu_ref[...], preferred_element_type=F32)
    o_ref[...] = (g * _sigmoid(g) * u).astype(o_ref.dtype)


def _ffn_up(x, wg, wu, tm, tn):
    m, k = x.shape
    n = wg.shape[1]
    return pl.pallas_call(
        _ffn_up_kernel,
        out_shape=jax.ShapeDtypeStruct((m, n), BF16),
        grid=(m // tm, n // tn),
        in_specs=[pl.BlockSpec((tm, k), lambda i, j: (i, 0)),
                  pl.BlockSpec((k, tn), lambda i, j: (0, j)),
                  pl.BlockSpec((k, tn), lambda i, j: (0, j))],
        out_specs=pl.BlockSpec((tm, tn), lambda i, j: (i, j)),
        compiler_params=_params(("parallel", "arbitrary")),
        name="ffn_up",
    )(x, wg, wu)


def _mm_res_ln_kernel(a_ref, w_ref, x_ref, g_ref, b_ref, of_ref, ob_ref, *, scale):
    kk = pl.program_id(1)

    @pl.when(kk == 0)
    def _():
        of_ref[...] = jnp.zeros_like(of_ref)

    of_ref[...] += jnp.dot(a_ref[...], w_ref[...], preferred_element_type=F32)

    @pl.when(kk == pl.num_programs(1) - 1)
    def _():
        z = DEEPNORM_ALPHA * x_ref[...] + scale * of_ref[...]
        mu = jnp.mean(z, axis=-1, keepdims=True)
        zc = z - mu
        var = jnp.mean(zc * zc, axis=-1, keepdims=True)
        y = zc * lax.rsqrt(var + LN_EPS) * g_ref[...] + b_ref[...]
        of_ref[...] = y
        ob_ref[...] = y.astype(BF16)


def _mm_res_ln(a, w, x, g, b, scale, tm, tk, name):
    m, k = a.shape
    n = w.shape[1]
    return pl.pallas_call(
        functools.partial(_mm_res_ln_kernel, scale=scale),
        out_shape=(jax.ShapeDtypeStruct((m, n), F32), jax.ShapeDtypeStruct((m, n), BF16)),
        grid=(m // tm, k // tk),
        in_specs=[pl.BlockSpec((tm, tk), lambda i, kk: (i, kk)),
                  pl.BlockSpec((tk, n), lambda i, kk: (kk, 0)),
                  pl.BlockSpec((tm, n), lambda i, kk: (i, 0), pipeline_mode=pl.Buffered(1)),
                  pl.BlockSpec((1, n), lambda i, kk: (0, 0)),
                  pl.BlockSpec((1, n), lambda i, kk: (0, 0))],
        out_specs=(pl.BlockSpec((tm, n), lambda i, kk: (i, 0)),
                   pl.BlockSpec((tm, n), lambda i, kk: (i, 0))),
        compiler_params=_params(("parallel", "arbitrary")),
        name=name,
    )(a, w, x, g, b)


def _gla_gate_kernel(c_ref, gw_ref, gb_ref, o_ref):
    z = jnp.dot(c_ref[...].astype(BF16), gw_ref[...], preferred_element_type=F32) + gb_ref[...]
    log_sig = jnp.minimum(z, 0.0) - jnp.log(1.0 + jnp.exp(-jnp.abs(z)))
    o_ref[...] = log_sig * (1.0 / GLA_NORMALIZER)


def _gla_gates(c, gw, gb, tm):
    m = c.shape[0]
    n = gw.shape[1]
    return pl.pallas_call(
        _gla_gate_kernel,
        out_shape=jax.ShapeDtypeStruct((m, n), F32),
        grid=(m // tm,),
        in_specs=[pl.BlockSpec((tm, LANE), lambda i: (i, C_GATE_BLOCK)),
                  pl.BlockSpec((LANE, n), lambda i: (0, 0)),
                  pl.BlockSpec((1, n), lambda i: (0, 0))],
        out_specs=pl.BlockSpec((tm, n), lambda i: (i, 0)),
        compiler_params=_params(("parallel",)),
        name="gla_gates",
    )(c, gw, gb)


def _gla_chunk(q, k, v, lg, st_ref, reverse):
    c, sub = GLA_CHUNK, GLA_SUB
    nsub = c // sub
    row = lax.broadcasted_iota(jnp.int32, (c, c), 0)
    col = lax.broadcasted_iota(jnp.int32, (c, c), 1)
    tri = jnp.where((col >= row) if reverse else (col <= row), 1.0, 0.0).astype(BF16)
    hi = lg.astype(BF16)
    r1 = lg - hi.astype(F32)
    mid = r1.astype(BF16)
    lo = (r1 - mid.astype(F32)).astype(BF16)
    b = (jnp.dot(tri, hi, preferred_element_type=F32) + jnp.dot(tri, mid, preferred_element_type=F32)
         + jnp.dot(tri, lo, preferred_element_type=F32))
    total = b[0:1, :] if reverse else b[c - 1:c, :]

    q = q * (GLA_DK ** -0.5)
    st = st_ref[...]
    o = lax.dot_general((q * jnp.exp(b)).astype(BF16), st.astype(BF16), (((1,), (1,)), ((), ())),
                        preferred_element_type=F32)
    k_state = (k * jnp.exp(total - b)).astype(BF16)
    st_ref[...] = st * jnp.exp(total) + lax.dot_general(
        v.astype(BF16), k_state, (((0,), (0,)), ((), ())), preferred_element_type=F32)

    lane = lax.broadcasted_iota(jnp.int32, (sub, c), 1)
    lrow = lax.broadcasted_iota(jnp.int32, (sub, c), 0)
    blocks = []
    for i_sub in range(nsub):
        r0 = i_sub * sub
        q_r = q[r0:r0 + sub, :]
        b_r = b[r0:r0 + sub, :]
        a_blk = jnp.zeros((sub, c), F32)
        has_off = (i_sub < nsub - 1) if reverse else (i_sub > 0)
        if has_off:
            edge = r0 + sub if reverse else r0 - 1
            ref = b[edge:edge + 1, :]
            q_t = (q_r * jnp.exp(b_r - ref)).astype(BF16)
            k_t = (k * jnp.exp(jnp.minimum(ref - b, 0.0))).astype(BF16)
            a_off = lax.dot_general(q_t, k_t, (((1,), (1,)), ((), ())), preferred_element_type=F32)
            keep = (lane >= r0 + sub) if reverse else (lane < r0)
            a_blk = jnp.where(keep, a_off, 0.0)
        for jj in range(sub):
            j = r0 + jj
            w = q_r * k[j:j + 1, :] * jnp.exp(jnp.minimum(b_r - b[j:j + 1, :], 0.0))
            s = jnp.sum(w, axis=1, keepdims=True)
            keep = (lrow < jj) if reverse else (lrow >= jj)
            a_blk = a_blk + jnp.where((lane == j) & keep, s, 0.0)
        blocks.append(a_blk)
    a = jnp.concatenate(blocks, axis=0)
    return o + jnp.dot(a.astype(BF16), v.astype(BF16), preferred_element_type=F32)


def _gla_fwd_kernel(q_ref, k_ref, v_ref, lg_ref, o_ref, st_ref, *, blocks_per_seq):
    @pl.when(pl.program_id(1) % blocks_per_seq == 0)
    def _():
        st_ref[...] = jnp.zeros_like(st_ref)

    def body(ci, carry):
        rows = pl.ds(pl.multiple_of(ci * GLA_CHUNK, GLA_CHUNK), GLA_CHUNK)
        o_ref[rows, :] = _gla_chunk(q_ref[rows, :], k_ref[rows, :], v_ref[rows, :], lg_ref[rows, :],
                                    st_ref, False)
        return carry

    lax.fori_loop(0, GLA_BLOCK // GLA_CHUNK, body, 0)


def _gla_bwd_kernel(q_ref, k_ref, v_ref, lg_ref, of_ref, r_ref, g_ref, y_ref, st_ref, *, blocks_per_seq):
    @pl.when(pl.program_id(1) % blocks_per_seq == 0)
    def _():
        st_ref[...] = jnp.zeros_like(st_ref)

    n_chunks = GLA_BLOCK // GLA_CHUNK

    def body(ci, carry):
        rows = pl.ds(pl.multiple_of((n_chunks - 1 - ci) * GLA_CHUNK, GLA_CHUNK), GLA_CHUNK)
        o = of_ref[rows, :] + _gla_chunk(q_ref[rows, :], k_ref[rows, :], v_ref[rows, :], lg_ref[rows, :],
                                         st_ref, True)
        o = o * lax.rsqrt(jnp.mean(o * o, axis=-1, keepdims=True) + RMS_EPS) * g_ref[...]
        r = r_ref[rows, :]
        y_ref[rows, :] = (o * (r * _sigmoid(r))).astype(y_ref.dtype)
        return carry

    lax.fori_loop(0, n_chunks, body, 0)


def _gla(pa, lg, norm_g, seq):
    m = pa.shape[0]
    nb = m // GLA_BLOCK
    bps = seq // GLA_BLOCK
    hk = GLA_HEADS
    blk = GLA_BLOCK

    def specs(row_of):
        return [pl.BlockSpec((blk, GLA_DK), lambda h, i: (row_of(i), h)),
                pl.BlockSpec((blk, GLA_DK), lambda h, i: (row_of(i), hk + h)),
                pl.BlockSpec((blk, GLA_DV), lambda h, i: (row_of(i), hk + h))]

    fwd_row = lambda i: i
    bwd_row = lambda i: nb - 1 - i
    scratch = [pltpu.VMEM((GLA_DV, GLA_DK), F32)]
    o_f = pl.pallas_call(
        functools.partial(_gla_fwd_kernel, blocks_per_seq=bps),
        out_shape=jax.ShapeDtypeStruct((m, GLA_HEADS * GLA_DV), F32),
        grid=(GLA_HEADS, nb),
        in_specs=specs(fwd_row) + [pl.BlockSpec((blk, GLA_DK), lambda h, i: (i, h))],
        out_specs=pl.BlockSpec((blk, GLA_DV), lambda h, i: (i, h)),
        scratch_shapes=scratch,
        compiler_params=_params(("parallel", "arbitrary")),
        name="gla_fwd",
    )(pa, pa, pa, lg)
    return pl.pallas_call(
        functools.partial(_gla_bwd_kernel, blocks_per_seq=bps),
        out_shape=jax.ShapeDtypeStruct((m, GLA_HEADS * GLA_DV), BF16),
        grid=(GLA_HEADS, nb),
        in_specs=specs(bwd_row) + [
            pl.BlockSpec((blk, GLA_DK), lambda h, i: (bwd_row(i), hk + h)),
            pl.BlockSpec((blk, GLA_DV), lambda h, i: (bwd_row(i), h)),
            pl.BlockSpec((blk, GLA_DV), lambda h, i: (bwd_row(i), 2 * hk + h)),
            pl.BlockSpec((1, GLA_DV), lambda h, i: (0, 0))],
        out_specs=pl.BlockSpec((blk, GLA_DV), lambda h, i: (bwd_row(i), h)),
        scratch_shapes=scratch,
        compiler_params=_params(("parallel", "arbitrary")),
        name="gla_bwd",
    )(pa, pa, pa, lg, o_f, pa, norm_g)


def _dil_kernel(q_ref, kp_ref, kc_ref, kn_ref, vp_ref, vc_ref, vn_ref, o_ref, l_ref, *, rows_per_seq):
    t, r = DIL_TILE, DIL_RADIUS
    start = pl.program_id(1) * t
    seq_lo = (start // rows_per_seq) * rows_per_seq
    row = lax.broadcasted_iota(jnp.int32, (t, t + 2 * r), 0)
    col = lax.broadcasted_iota(jnp.int32, (t, t + 2 * r), 1)
    rel = col - row
    pos = start - r + col
    valid = (rel >= 0) & (rel <= 2 * r) & (pos >= seq_lo) & (pos < seq_lo + rows_per_seq)
    for h in range(DIL_HEADS):
        sl = slice(h * DIL_HD, (h + 1) * DIL_HD)
        kw = jnp.concatenate([kp_ref[:, sl], kc_ref[:, sl], kn_ref[:, sl]], axis=0)
        vw = jnp.concatenate([vp_ref[:, sl], vc_ref[:, sl], vn_ref[:, sl]], axis=0)
        s = lax.dot_general(q_ref[:, sl], kw, (((1,), (1,)), ((), ())), preferred_element_type=F32)
        s = jnp.where(valid, s, NEG_INF)
        m = jnp.max(s, axis=-1, keepdims=True)
        p = jnp.exp(s - m)
        den = jnp.sum(p, axis=-1, keepdims=True)
        o_ref[:, sl] = jnp.dot(p.astype(BF16), vw, preferred_element_type=F32) / den
        l_ref[:, sl] = jnp.broadcast_to(m + jnp.log(den), (t, DIL_HD))


def _dilated_group(qk, vv, g, seq):
    d = DIL_DILATIONS[g]
    m = qk.shape[0]
    rows = m // d
    t, r = DIL_TILE, DIL_RADIUS
    w = DIL_HEADS * DIL_HD
    ng = len(DIL_DILATIONS)
    qk_v = qk.reshape(rows, d * 2 * ng * w)
    vv_v = vv.reshape(rows, d * ng * w)
    sub = t // r
    last = rows // r - 1

    def prev_blk(i):
        return jnp.maximum(i * sub - 1, 0)

    def next_blk(i):
        return jnp.minimum((i + 1) * sub, last)

    kcol = lambda res: res * 2 * ng + ng + g
    vcol = lambda res: res * ng + g
    o, lse = pl.pallas_call(
        functools.partial(_dil_kernel, rows_per_seq=seq // d),
        out_shape=(jax.ShapeDtypeStruct((rows, d * w), F32), jax.ShapeDtypeStruct((rows, d * w), F32)),
        grid=(d, rows // t),
        in_specs=[pl.BlockSpec((t, w), lambda res, i: (i, res * 2 * ng + g)),
                  pl.BlockSpec((r, w), lambda res, i: (prev_blk(i), kcol(res))),
                  pl.BlockSpec((t, w), lambda res, i: (i, kcol(res))),
                  pl.BlockSpec((r, w), lambda res, i: (next_blk(i), kcol(res))),
                  pl.BlockSpec((r, w), lambda res, i: (prev_blk(i), vcol(res))),
                  pl.BlockSpec((t, w), lambda res, i: (i, vcol(res))),
                  pl.BlockSpec((r, w), lambda res, i: (next_blk(i), vcol(res)))],
        out_specs=(pl.BlockSpec((t, w), lambda res, i: (i, res)),
                   pl.BlockSpec((t, w), lambda res, i: (i, res))),
        compiler_params=_params(("parallel", "arbitrary")),
        name=f"dilated_{g}",
    )(qk_v, qk_v, qk_v, qk_v, vv_v, vv_v, vv_v)
    return o.reshape(m, w), lse.reshape(m, w)


def _dil_combine_kernel(o0, o1, o2, l0, l1, l2, y_ref):
    a, b, c = l0[...], l1[...], l2[...]
    mx = jnp.maximum(jnp.maximum(a, b), c)
    ea, eb, ec = jnp.exp(a - mx), jnp.exp(b - mx), jnp.exp(c - mx)
    y_ref[...] = ((ea * o0[...] + eb * o1[...] + ec * o2[...]) / (ea + eb + ec)).astype(y_ref.dtype)


def _dil_combine(outs, lses, tm):
    m, w = outs[0].shape
    spec = pl.BlockSpec((tm, w), lambda i: (i, 0))
    return pl.pallas_call(
        _dil_combine_kernel,
        out_shape=jax.ShapeDtypeStruct((m, w), BF16),
        grid=(m // tm,),
        in_specs=[spec] * 6,
        out_specs=spec,
        compiler_params=_params(("parallel",)),
        name="dilated_combine",
    )(*outs, *lses)


def _rope64(x, cos, sin_a, sin_b):
    return x * cos + pltpu.roll(x, 96, 1) * sin_a + pltpu.roll(x, 32, 1) * sin_b


def _mla_post_kernel(c_ref, gq_ref, gkv_ref, wuq_ref, wkn_ref, wv_ref, cos_ref, sa_ref, sb_ref,
                     q_ref, k_ref, v_ref):
    c = c_ref[...]
    cq = c[:, :MLA_QRANK]
    cq = cq * lax.rsqrt(jnp.mean(cq * cq, axis=-1, keepdims=True) + RMS_EPS) * gq_ref[...]
    ckv = c[:, MLA_QRANK:MLA_QRANK + MLA_KVRANK]
    ckv = (ckv * lax.rsqrt(jnp.mean(ckv * ckv, axis=-1, keepdims=True) + RMS_EPS) * gkv_ref[...]).astype(BF16)
    cos, sa, sb = cos_ref[...], sa_ref[...], sb_ref[...]
    scale = (MLA_NOPE + MLA_ROPE) ** -0.5
    q = jnp.dot(cq.astype(BF16), wuq_ref[...], preferred_element_type=F32) * scale
    kn = jnp.dot(ckv, wkn_ref[...], preferred_element_type=F32)
    kr = _rope64(c[:, MLA_QRANK + MLA_KVRANK:MLA_QRANK + MLA_KVRANK + LANE], cos, sa, sb).astype(BF16)
    for h in range(MLA_HEADS):
        base = h * MLA_HPAD
        q_ref[:, base:base + LANE] = q[:, base:base + LANE].astype(BF16)
        q_ref[:, base + LANE:base + 2 * LANE] = _rope64(q[:, base + LANE:base + 2 * LANE], cos, sa, sb).astype(BF16)
        k_ref[:, base:base + LANE] = kn[:, h * LANE:(h + 1) * LANE].astype(BF16)
        k_ref[:, base + LANE:base + 2 * LANE] = kr
    v_ref[...] = jnp.dot(ckv, wv_ref[...], preferred_element_type=F32).astype(BF16)


def _mla_post(c, gq, gkv, wuq, wkn, wv, cos, sa, sb, seq, tm):
    m = c.shape[0]
    tiles_per_seq = seq // tm
    const = lambda i: (0, 0)
    tab = pl.BlockSpec((tm, LANE), lambda i: (i % tiles_per_seq, 0))
    hw = MLA_HEADS * MLA_HPAD
    return pl.pallas_call(
        _mla_post_kernel,
        out_shape=(jax.ShapeDtypeStruct((m, hw), BF16), jax.ShapeDtypeStruct((m, hw), BF16),
                   jax.ShapeDtypeStruct((m, MLA_HEADS * MLA_V), BF16)),
        grid=(m // tm,),
        in_specs=[pl.BlockSpec((tm, C_COLS), lambda i: (i, 0)),
                  pl.BlockSpec((1, MLA_QRANK), const), pl.BlockSpec((1, MLA_KVRANK), const),
                  pl.BlockSpec(wuq.shape, const), pl.BlockSpec(wkn.shape, const), pl.BlockSpec(wv.shape, const),
                  tab, tab, tab],
        out_specs=(pl.BlockSpec((tm, hw), lambda i: (i, 0)), pl.BlockSpec((tm, hw), lambda i: (i, 0)),
                   pl.BlockSpec((tm, MLA_HEADS * MLA_V), lambda i: (i, 0))),
        compiler_params=_params(("parallel",)),
        name="mla_post",
    )(c, gq, gkv, wuq, wkn, wv, cos, sa, sb)


def _mla_attn_kernel(q_ref, k_ref, v_ref, o_ref, m_ref, l_ref, acc_ref, *, tk):
    q = q_ref[...]
    m_ref[...] = jnp.full_like(m_ref, -jnp.inf)
    l_ref[...] = jnp.zeros_like(l_ref)
    acc_ref[...] = jnp.zeros_like(acc_ref)

    def body(ci, carry):
        rows = pl.ds(pl.multiple_of(ci * tk, tk), tk)
        s = lax.dot_general(q, k_ref[rows, :], (((1,), (1,)), ((), ())), preferred_element_type=F32)
        m_prev = m_ref[...]
        m_new = jnp.maximum(m_prev, jnp.max(s, axis=-1, keepdims=True))
        alpha = jnp.exp(m_prev - m_new)
        p = jnp.exp(s - m_new)
        l_ref[...] = alpha * l_ref[...] + jnp.sum(p, axis=-1, keepdims=True)
        acc_ref[...] = alpha * acc_ref[...] + jnp.dot(p.astype(BF16), v_ref[rows, :], preferred_element_type=F32)
        m_ref[...] = m_new
        return carry

    lax.fori_loop(0, k_ref.shape[0] // tk, body, 0)
    o_ref[...] = (acc_ref[...] / l_ref[...]).astype(o_ref.dtype)


def _mla_attn(q, k, v, seq, tq, tk):
    m = q.shape[0]
    nseq = m // seq
    qt = seq // tq
    return pl.pallas_call(
        functools.partial(_mla_attn_kernel, tk=tk),
        out_shape=jax.ShapeDtypeStruct((m, MLA_HEADS * MLA_V), BF16),
        grid=(nseq, MLA_HEADS, qt),
        in_specs=[pl.BlockSpec((tq, MLA_HPAD), lambda b, h, i: (b * qt + i, h)),
                  pl.BlockSpec((seq, MLA_HPAD), lambda b, h, i: (b, h)),
                  pl.BlockSpec((seq, MLA_V), lambda b, h, i: (b, h))],
        out_specs=pl.BlockSpec((tq, MLA_V), lambda b, h, i: (b * qt + i, h)),
        scratch_shapes=[pltpu.VMEM((tq, 1), F32), pltpu.VMEM((tq, 1), F32), pltpu.VMEM((tq, MLA_V), F32)],
        compiler_params=_params(("parallel", "parallel", "arbitrary")),
        name="mla_attn",
    )(q, k, v)


def _merge_kernel(x_ref, wg0, wg1, wg2, ya, yb, yc, wb0, wb1, wb2, o_ref):
    x = x_ref[...]
    acc = None
    for wg, y, wb in ((wg0, ya, wb0), (wg1, yb, wb1), (wg2, yc, wb2)):
        gate = _sigmoid(jnp.dot(x, wg[...], preferred_element_type=F32))
        term = gate * jnp.dot(y[...], wb[...], preferred_element_type=F32)
        acc = term if acc is None else acc + term
    o_ref[...] = acc.astype(o_ref.dtype)


def _merge(x, w_gate, ys, w_branch, tm, tn):
    m, k = x.shape
    n = w_branch.shape[2]
    nj = n // tn
    kb = w_branch.shape[1]
    gate_specs = [pl.BlockSpec((k, tn), functools.partial(lambda i, j, b: (0, b * nj + j), b=b)) for b in range(3)]
    y_specs = [pl.BlockSpec((tm, kb), lambda i, j: (i, 0))] * 3
    br_specs = [pl.BlockSpec((None, kb, tn), functools.partial(lambda i, j, b: (b, 0, j), b=b)) for b in range(3)]
    return pl.pallas_call(
        _merge_kernel,
        out_shape=jax.ShapeDtypeStruct((m, n), BF16),
        grid=(m // tm, nj),
        in_specs=[pl.BlockSpec((tm, k), lambda i, j: (i, 0))] + gate_specs + y_specs + br_specs,
        out_specs=pl.BlockSpec((tm, tn), lambda i, j: (i, j)),
        compiler_params=_params(("parallel", "arbitrary")),
        name="merge",
    )(x, w_gate, w_gate, w_gate, *ys, w_branch, w_branch, w_branch)


def _prep_layer(w_in, gate_w, gate_b, norm_g, q_norm_g, uq, kv_norm_g, ukv, w_branch, w_out,
                ffn_g, ffn_u, ffn_d, ln_g, ln_b):
    d = D_MODEL
    n_a = 2 * GLA_HEADS * GLA_DK + 2 * GLA_HEADS * GLA_DV
    n_rank = 2 * GLA_RANK
    n_b = len(DIL_DILATIONS) * 3 * BRANCH_W
    n_c = MLA_QRANK + MLA_KVRANK + MLA_ROPE
    o_b = n_a + n_rank
    o_c = o_b + n_b
    o_g = o_c + n_c
    w_b = w_in[:, o_b:o_c].reshape(d, len(DIL_DILATIONS), 3, BRANCH_W)
    w_c = jnp.concatenate([w_in[:, o_c:o_g], jnp.zeros((d, LANE - MLA_ROPE), F32),
                           w_in[:, n_a:o_b], jnp.zeros((d, LANE - n_rank), F32)], axis=1)
    gw = jnp.zeros((LANE, 2 * GLA_HEADS * GLA_DK), F32)
    gw = gw.at[:GLA_RANK, :GLA_HEADS * GLA_DK].set(gate_w[0])
    gw = gw.at[GLA_RANK:2 * GLA_RANK, GLA_HEADS * GLA_DK:].set(gate_w[1])
    uq_pad = jnp.pad(uq.reshape(MLA_QRANK, MLA_HEADS, MLA_NOPE + MLA_ROPE),
                     ((0, 0), (0, 0), (0, MLA_HPAD - MLA_NOPE - MLA_ROPE))).reshape(MLA_QRANK, MLA_HEADS * MLA_HPAD)
    ukv3 = ukv.reshape(MLA_KVRANK, MLA_HEADS, MLA_NOPE + MLA_V)
    fpad = D_FF_PAD - D_FF
    return dict(
        w_a=w_in[:, :n_a].astype(BF16),
        w_qk=jnp.transpose(w_b[:, :, :2], (0, 2, 1, 3)).reshape(d, -1).astype(BF16),
        w_v=w_b[:, :, 2].reshape(d, -1).astype(BF16),
        w_c=w_c.astype(BF16),
        w_gate=w_in[:, o_g:].astype(BF16),
        gla_gw=gw.astype(BF16),
        gla_gb=jnp.concatenate([gate_b[0], gate_b[1]])[None, :],
        gla_norm=norm_g[None, :],
        q_norm=q_norm_g[None, :],
        kv_norm=kv_norm_g[None, :],
        w_uq=uq_pad.astype(BF16),
        w_kn=ukv3[:, :, :MLA_NOPE].reshape(MLA_KVRANK, -1).astype(BF16),
        w_vv=ukv3[:, :, MLA_NOPE:].reshape(MLA_KVRANK, -1).astype(BF16),
        w_branch=w_branch.astype(BF16),
        w_out=w_out.astype(BF16),
        ffn_g=jnp.pad(ffn_g, ((0, 0), (0, 0), (0, fpad))).astype(BF16),
        ffn_u=jnp.pad(ffn_u, ((0, 0), (0, 0), (0, fpad))).astype(BF16),
        ffn_d=jnp.pad(ffn_d, ((0, 0), (0, fpad), (0, 0))).astype(BF16),
        ln_g=ln_g[:, None, :],
        ln_b=ln_b[:, None, :],
    )


def _rope_tables(seq):
    pos = jnp.arange(seq, dtype=F32)[:, None]
    half = DIL_HD // 2
    ang = pos * jnp.power(ROPE_THETA, -2.0 * jnp.arange(half, dtype=F32) / DIL_HD)[None, :]
    cos128 = jnp.concatenate([jnp.cos(ang), jnp.cos(ang)], axis=1)
    sin128 = jnp.concatenate([-jnp.sin(ang), jnp.sin(ang)], axis=1)
    half = MLA_ROPE // 2
    ang = pos * jnp.power(ROPE_THETA, -2.0 * jnp.arange(half, dtype=F32) / MLA_ROPE)[None, :]
    zero = jnp.zeros((seq, half), F32)
    pad = jnp.zeros((seq, LANE - MLA_ROPE), F32)
    cos64 = jnp.concatenate([jnp.cos(ang), jnp.cos(ang), pad], axis=1)
    sin_a = jnp.concatenate([-jnp.sin(ang), zero, pad], axis=1)
    sin_b = jnp.concatenate([zero, jnp.sin(ang), pad], axis=1)
    return cos128, sin128, cos64, sin_a, sin_b


def _layer(x, xb, p, tabs, seq):
    cos128, sin128, cos64, sin_a, sin_b = tabs
    h = _ffn_up(xb, p["ffn_g"][0], p["ffn_u"][0], 1024, 256)
    x, xb = _mm_res_ln(h, p["ffn_d"][0], x, p["ln_g"][0], p["ln_b"][0], 0.5, 512, 512, "ffn_down_ln")

    pa = _proj(xb, p["w_a"], F32, 1024, 512, "proj_gla")
    qk = _proj_rope(xb, p["w_qk"], cos128, sin128, seq, len(DIL_DILATIONS) * BRANCH_W, DIL_HD ** -0.5,
                    1024, 512, "proj_dil_qk")
    vv = _proj(xb, p["w_v"], BF16, 1024, 512, "proj_dil_v")
    c = _proj(xb, p["w_c"], F32, 512, C_COLS, "proj_c")

    lg = _gla_gates(c, p["gla_gw"], p["gla_gb"], 512)
    y_a = _gla(pa, lg, p["gla_norm"], seq)

    outs, lses = zip(*[_dilated_group(qk, vv, g, seq) for g in range(len(DIL_DILATIONS))])
    y_b = _dil_combine(outs, lses, 256)

    q_pad, k_pad, v_m = _mla_post(c, p["q_norm"], p["kv_norm"], p["w_uq"], p["w_kn"], p["w_vv"],
                                  cos64, sin_a, sin_b, seq, 256)
    y_c = _mla_attn(q_pad, k_pad, v_m, seq, 512, 512)

    merged = _merge(xb, p["w_gate"], (y_a, y_b, y_c), p["w_branch"], 512, 256)
    x, xb = _mm_res_ln(merged, p["w_out"], x, p["ln_g"][1], p["ln_b"][1], 1.0, 512, 512, "out_proj_ln")

    h = _ffn_up(xb, p["ffn_g"][1], p["ffn_u"][1], 1024, 256)
    return _mm_res_ln(h, p["ffn_d"][1], x, p["ln_g"][2], p["ln_b"][2], 0.5, 512, 512, "ffn_down_ln")


def _trunk(x3, layers):
    bsz, seq, d = x3.shape
    x = x3.reshape(bsz * seq, d)
    xb = x.astype(BF16)
    tabs = _rope_tables(seq)
    for p in layers:
        x, xb = _layer(x, xb, p, tabs, seq)
    return x.reshape(bsz, seq, d)


def kernel(x_prompt, x_sample, w_in, gla_gate_w, gla_gate_b, gla_norm_g, mla_q_norm_g, mla_uq, mla_kv_norm_g,
           mla_ukv, w_branch, w_out, ffn_w_gate, ffn_w_up, ffn_w_down, ln_g, ln_b):
    layers = [_prep_layer(w_in[l], gla_gate_w[l], gla_gate_b[l], gla_norm_g[l], mla_q_norm_g[l], mla_uq[l],
                          mla_kv_norm_g[l], mla_ukv[l], w_branch[l], w_out[l], ffn_w_gate[l], ffn_w_up[l],
                          ffn_w_down[l], ln_g[l], ln_b[l]) for l in range(N_LAYERS)]
    return (_trunk(x_prompt, layers), _trunk(x_sample, layers))
```

```python
import functools
import math

import jax
import jax.numpy as jnp
from jax import lax
from jax.experimental import pallas as pl
from jax.experimental.pallas import tpu as pltpu

F32 = jnp.float32
BF16 = jnp.bfloat16

D_MODEL = 4096
N_LAYERS = 2
D_FF = 11008
D_FF_PAD = 11264
ROPE_THETA = 10000.0
LN_EPS = 1e-5
RMS_EPS = 1e-6
NEG_INF = -1e30
DEEPNORM_ALPHA = (2 * N_LAYERS) ** 0.25
BRANCH_W = 2048

GLA_HEADS = 4
GLA_DK = 256
GLA_DV = 512
GLA_RANK = 16
GLA_NORMALIZER = 16.0
GLA_CHUNK = 64
GLA_SUB = 16
GLA_BLOCK = 256

DIL_DILATIONS = (1, 4, 16)
DIL_RADIUS = 64
DIL_HEADS = 16
DIL_HD = 128
DIL_TILE = 256

MLA_HEADS = 16
MLA_NOPE = 128
MLA_ROPE = 64
MLA_V = 128
MLA_QRANK = 896
MLA_KVRANK = 256
MLA_HPAD = 256
C_COLS = 1408
C_GATE_BLOCK = 10

LANE = 128
VMEM_LIMIT = 56 * 1024 * 1024


def _params(sem, vmem=VMEM_LIMIT):
    return pltpu.CompilerParams(dimension_semantics=sem, vmem_limit_bytes=vmem)


def _sigmoid(x):
    return 1.0 / (1.0 + jnp.exp(-x))


def _proj_kernel(x_ref, w_ref, o_ref):
    o_ref[...] = jnp.dot(x_ref[...], w_ref[...], preferred_element_type=F32).astype(o_ref.dtype)


def _proj(x, w, out_dtype, tm, tn, name):
    m, k = x.shape
    n = w.shape[1]
    return pl.pallas_call(
        _proj_kernel,
        out_shape=jax.ShapeDtypeStruct((m, n), out_dtype),
        grid=(m // tm, n // tn),
        in_specs=[pl.BlockSpec((tm, k), lambda i, j: (i, 0)),
                  pl.BlockSpec((k, tn), lambda i, j: (0, j))],
        out_specs=pl.BlockSpec((tm, tn), lambda i, j: (i, j)),
        compiler_params=_params(("parallel", "arbitrary")),
        name=name,
    )(x, w)


def _proj_dil_kernel(x_ref, w_ref, cos_ref, sin_ref, o_ref, stage_ref, *, d, n_q_tiles, n_rope_tiles, q_scale):
    acc = jnp.dot(x_ref[...], w_ref[...], preferred_element_type=F32)
    j = pl.program_id(1)
    rows = acc.shape[0] // d
    n_heads = acc.shape[1] // LANE

    @pl.when(j < n_rope_tiles)
    def _():
        scale = jnp.where(j < n_q_tiles, q_scale, 1.0).astype(F32)
        cos = cos_ref[...] * scale
        sin = sin_ref[...] * scale
        for h in range(n_heads):
            y = acc[:, h * LANE:(h + 1) * LANE]
            stage_ref[h] = y * cos + pltpu.roll(y, LANE // 2, 1) * sin

    @pl.when(j >= n_rope_tiles)
    def _():
        for h in range(n_heads):
            stage_ref[h] = acc[:, h * LANE:(h + 1) * LANE]

    for r in range(d):
        for h in range(n_heads):
            o_ref[r, :, h * LANE:(h + 1) * LANE] = stage_ref[h, pl.ds(r, rows, stride=d), :].astype(o_ref.dtype)


def _proj_dil(x, w, cos, sin, seq, d, tm, tn, name):
    m, k = x.shape
    n = w.shape[1]
    tiles_per_seq = seq // tm
    w_head = DIL_HEADS * DIL_HD
    return pl.pallas_call(
        functools.partial(_proj_dil_kernel, d=d, n_q_tiles=w_head // tn, n_rope_tiles=2 * w_head // tn,
                          q_scale=DIL_HD ** -0.5),
        out_shape=jax.ShapeDtypeStruct((d, m // d, n), BF16),
        grid=(m // tm, n // tn),
        in_specs=[pl.BlockSpec((tm, k), lambda i, j: (i, 0)),
                  pl.BlockSpec((k, tn), lambda i, j: (0, j)),
                  pl.BlockSpec((tm, LANE), lambda i, j: (i % tiles_per_seq, 0)),
                  pl.BlockSpec((tm, LANE), lambda i, j: (i % tiles_per_seq, 0))],
        out_specs=pl.BlockSpec((d, tm // d, tn), lambda i, j: (0, i, j)),
        scratch_shapes=[pltpu.VMEM((tn // LANE, tm, LANE), F32)],
        compiler_params=_params(("parallel", "arbitrary")),
        name=name,
    )(x, w, cos, sin)


def _ffn_up_kernel(x_ref, wg_ref, wu_ref, o_ref):
    x = x_ref[...]
    g = jnp.dot(x, wg_ref[...], preferred_element_type=F32)
    u = jnp.dot(x, wu_ref[...], preferred_element_type=F32)
    o_ref[...] = (g * _sigmoid(g) * u).astype(o_ref.dtype)


def _ffn_up(x, wg, wu, tm, tn):
    m, k = x.shape
    n = wg.shape[1]
    return pl.pallas_call(
        _ffn_up_kernel,
        out_shape=jax.ShapeDtypeStruct((m, n), BF16),
        grid=(m // tm, n // tn),
        in_specs=[pl.BlockSpec((tm, k), lambda i, j: (i, 0)),
                  pl.BlockSpec((k, tn), lambda i, j: (0, j)),
                  pl.BlockSpec((k, tn), lambda i, j: (0, j))],
        out_specs=pl.BlockSpec((tm, tn), lambda i, j: (i, j)),
        compiler_params=_params(("parallel", "arbitrary")),
        name="ffn_up",
    )(x, wg, wu)


def _mm_res_ln_kernel(a_ref, w_ref, x_ref, g_ref, b_ref, of_ref, ob_ref, *, scale):
    kk = pl.program_id(1)

    @pl.when(kk == 0)
    def _():
        of_ref[...] = jnp.zeros_like(of_ref)

    of_ref[...] += jnp.dot(a_ref[...], w_ref[...], preferred_element_type=F32)

    @pl.when(kk == pl.num_programs(1) - 1)
    def _():
        z = DEEPNORM_ALPHA * x_ref[...] + scale * of_ref[...]
        mu = jnp.mean(z, axis=-1, keepdims=True)
        zc = z - mu
        var = jnp.mean(zc * zc, axis=-1, keepdims=True)
        y = zc * lax.rsqrt(var + LN_EPS) * g_ref[...] + b_ref[...]
        of_ref[...] = y
        ob_ref[...] = y.astype(BF16)


def _mm_res_ln(a, w, x, g, b, scale, tm, tk, name):
    m, k = a.shape
    n = w.shape[1]
    return pl.pallas_call(
        functools.partial(_mm_res_ln_kernel, scale=scale),
        out_shape=(jax.ShapeDtypeStruct((m, n), F32), jax.ShapeDtypeStruct((m, n), BF16)),
        grid=(m // tm, k // tk),
        in_specs=[pl.BlockSpec((tm, tk), lambda i, kk: (i, kk)),
                  pl.BlockSpec((tk, n), lambda i, kk: (kk, 0)),
                  pl.BlockSpec((tm, n), lambda i, kk: (i, 0), pipeline_mode=pl.Buffered(1)),
                  pl.BlockSpec((1, n), lambda i, kk: (0, 0)),
                  pl.BlockSpec((1, n), lambda i, kk: (0, 0))],
        out_specs=(pl.BlockSpec((tm, n), lambda i, kk: (i, 0)),
                   pl.BlockSpec((tm, n), lambda i, kk: (i, 0))),
        compiler_params=_params(("parallel", "arbitrary")),
        name=name,
    )(a, w, x, g, b)


def _gla_gate_kernel(c_ref, gw_ref, gb_ref, o_ref):
    z = jnp.dot(c_ref[...].astype(BF16), gw_ref[...], preferred_element_type=F32) + gb_ref[...]
    log_sig = jnp.minimum(z, 0.0) - jnp.log(1.0 + jnp.exp(-jnp.abs(z)))
    o_ref[...] = log_sig * (1.0 / GLA_NORMALIZER)


def _gla_gates(c, gw, gb, tm):
    m = c.shape[0]
    n = gw.shape[1]
    return pl.pallas_call(
        _gla_gate_kernel,
        out_shape=jax.ShapeDtypeStruct((m, n), F32),
        grid=(m // tm,),
        in_specs=[pl.BlockSpec((tm, LANE), lambda i: (i, C_GATE_BLOCK)),
                  pl.BlockSpec((LANE, n), lambda i: (0, 0)),
                  pl.BlockSpec((1, n), lambda i: (0, 0))],
        out_specs=pl.BlockSpec((tm, n), lambda i: (i, 0)),
        compiler_params=_params(("parallel",)),
        name="gla_gates",
    )(c, gw, gb)


def _gla_chunk(q, k, v, lg, st_ref, reverse):
    c, sub = GLA_CHUNK, GLA_SUB
    nsub = c // sub
    row = lax.broadcasted_iota(jnp.int32, (c, c), 0)
    col = lax.broadcasted_iota(jnp.int32, (c, c), 1)
    tri = jnp.where((col >= row) if reverse else (col <= row), 1.0, 0.0).astype(BF16)
    hi = lg.astype(BF16)
    r1 = lg - hi.astype(F32)
    mid = r1.astype(BF16)
    lo = (r1 - mid.astype(F32)).astype(BF16)
    b = (jnp.dot(tri, hi, preferred_element_type=F32) + jnp.dot(tri, mid, preferred_element_type=F32)
         + jnp.dot(tri, lo, preferred_element_type=F32))
    total = b[0:1, :] if reverse else b[c - 1:c, :]

    q = q * (GLA_DK ** -0.5)
    st = st_ref[...]
    o = lax.dot_general((q * jnp.exp(b)).astype(BF16), st.astype(BF16), (((1,), (1,)), ((), ())),
                        preferred_element_type=F32)
    k_state = (k * jnp.exp(total - b)).astype(BF16)
    st_ref[...] = st * jnp.exp(total) + lax.dot_general(
        v.astype(BF16), k_state, (((0,), (0,)), ((), ())), preferred_element_type=F32)

    lane = lax.broadcasted_iota(jnp.int32, (sub, c), 1)
    lrow = lax.broadcasted_iota(jnp.int32, (sub, c), 0)
    blocks = []
    for i_sub in range(nsub):
        r0 = i_sub * sub
        q_r = q[r0:r0 + sub, :]
        b_r = b[r0:r0 + sub, :]
        a_blk = jnp.zeros((sub, c), F32)
        has_off = (i_sub < nsub - 1) if reverse else (i_sub > 0)
        if has_off:
            edge = r0 + sub if reverse else r0 - 1
            ref = b[edge:edge + 1, :]
            q_t = (q_r * jnp.exp(b_r - ref)).astype(BF16)
            k_t = (k * jnp.exp(jnp.minimum(ref - b, 0.0))).astype(BF16)
            a_off = lax.dot_general(q_t, k_t, (((1,), (1,)), ((), ())), preferred_element_type=F32)
            keep = (lane >= r0 + sub) if reverse else (lane < r0)
            a_blk = jnp.where(keep, a_off, 0.0)
        for jj in range(sub):
            j = r0 + jj
            w = q_r * k[j:j + 1, :] * jnp.exp(jnp.minimum(b_r - b[j:j + 1, :], 0.0))
            s = jnp.sum(w, axis=1, keepdims=True)
            keep = (lrow < jj) if reverse else (lrow >= jj)
            a_blk = a_blk + jnp.where((lane == j) & keep, s, 0.0)
        blocks.append(a_blk)
    a = jnp.concatenate(blocks, axis=0)
    return o + jnp.dot(a.astype(BF16), v.astype(BF16), preferred_element_type=F32)


def _gla_fwd_kernel(q_ref, k_ref, v_ref, lg_ref, o_ref, st_ref, *, blocks_per_seq):
    @pl.when(pl.program_id(1) % blocks_per_seq == 0)
    def _():
        st_ref[...] = jnp.zeros_like(st_ref)

    def body(ci, carry):
        rows = pl.ds(pl.multiple_of(ci * GLA_CHUNK, GLA_CHUNK), GLA_CHUNK)
        o_ref[rows, :] = _gla_chunk(q_ref[rows, :], k_ref[rows, :], v_ref[rows, :], lg_ref[rows, :],
                                    st_ref, False)
        return carry

    lax.fori_loop(0, GLA_BLOCK // GLA_CHUNK, body, 0)


def _gla_bwd_kernel(q_ref, k_ref, v_ref, lg_ref, of_ref, r_ref, g_ref, y_ref, st_ref, *, blocks_per_seq):
    @pl.when(pl.program_id(1) % blocks_per_seq == 0)
    def _():
        st_ref[...] = jnp.zeros_like(st_ref)

    n_chunks = GLA_BLOCK // GLA_CHUNK

    def body(ci, carry):
        rows = pl.ds(pl.multiple_of((n_chunks - 1 - ci) * GLA_CHUNK, GLA_CHUNK), GLA_CHUNK)
        o = of_ref[rows, :] + _gla_chunk(q_ref[rows, :], k_ref[rows, :], v_ref[rows, :], lg_ref[rows, :],
                                         st_ref, True)
        o = o * lax.rsqrt(jnp.mean(o * o, axis=-1, keepdims=True) + RMS_EPS) * g_ref[...]
        r = r_ref[rows, :]
        y_ref[rows, :] = (o * (r * _sigmoid(r))).astype(y_ref.dtype)
        return carry

    lax.fori_loop(0, n_chunks, body, 0)


def _gla(pa, lg, norm_g, seq):
    m = pa.shape[0]
    nb = m // GLA_BLOCK
    bps = seq // GLA_BLOCK
    hk = GLA_HEADS
    blk = GLA_BLOCK

    def specs(row_of):
        return [pl.BlockSpec((blk, GLA_DK), lambda h, i: (row_of(i), h)),
                pl.BlockSpec((blk, GLA_DK), lambda h, i: (row_of(i), hk + h)),
                pl.BlockSpec((blk, GLA_DV), lambda h, i: (row_of(i), hk + h))]

    fwd_row = lambda i: i
    bwd_row = lambda i: nb - 1 - i
    scratch = [pltpu.VMEM((GLA_DV, GLA_DK), F32)]
    o_f = pl.pallas_call(
        functools.partial(_gla_fwd_kernel, blocks_per_seq=bps),
        out_shape=jax.ShapeDtypeStruct((m, GLA_HEADS * GLA_DV), F32),
        grid=(GLA_HEADS, nb),
        in_specs=specs(fwd_row) + [pl.BlockSpec((blk, GLA_DK), lambda h, i: (i, h))],
        out_specs=pl.BlockSpec((blk, GLA_DV), lambda h, i: (i, h)),
        scratch_shapes=scratch,
        compiler_params=_params(("parallel", "arbitrary")),
        name="gla_fwd",
    )(pa, pa, pa, lg)
    return pl.pallas_call(
        functools.partial(_gla_bwd_kernel, blocks_per_seq=bps),
        out_shape=jax.ShapeDtypeStruct((m, GLA_HEADS * GLA_DV), BF16),
        grid=(GLA_HEADS, nb),
        in_specs=specs(bwd_row) + [
            pl.BlockSpec((blk, GLA_DK), lambda h, i: (bwd_row(i), hk + h)),
            pl.BlockSpec((blk, GLA_DV), lambda h, i: (bwd_row(i), h)),
            pl.BlockSpec((blk, GLA_DV), lambda h, i: (bwd_row(i), 2 * hk + h)),
            pl.BlockSpec((1, GLA_DV), lambda h, i: (0, 0))],
        out_specs=pl.BlockSpec((blk, GLA_DV), lambda h, i: (bwd_row(i), h)),
        scratch_shapes=scratch,
        compiler_params=_params(("parallel", "arbitrary")),
        name="gla_bwd",
    )(pa, pa, pa, lg, o_f, pa, norm_g)


def _dil_kernel(q_ref, kp_ref, kc_ref, kn_ref, vp_ref, vc_ref, vn_ref, o_ref, l_ref, *, rows_per_seq):
    t, r = DIL_TILE, DIL_RADIUS
    start = pl.program_id(1) * t
    seq_lo = (start // rows_per_seq) * rows_per_seq
    row = lax.broadcasted_iota(jnp.int32, (t, t + 2 * r), 0)
    col = lax.broadcasted_iota(jnp.int32, (t, t + 2 * r), 1)
    rel = col - row
    pos = start - r + col
    valid = (rel >= 0) & (rel <= 2 * r) & (pos >= seq_lo) & (pos < seq_lo + rows_per_seq)
    for h in range(DIL_HEADS):
        sl = slice(h * DIL_HD, (h + 1) * DIL_HD)
        kw = jnp.concatenate([kp_ref[:, sl], kc_ref[:, sl], kn_ref[:, sl]], axis=0)
        vw = jnp.concatenate([vp_ref[:, sl], vc_ref[:, sl], vn_ref[:, sl]], axis=0)
        s = lax.dot_general(q_ref[:, sl], kw, (((1,), (1,)), ((), ())), preferred_element_type=F32)
        s = jnp.where(valid, s, NEG_INF)
        m = jnp.max(s, axis=-1, keepdims=True)
        p = jnp.exp(s - m)
        den = jnp.sum(p, axis=-1, keepdims=True)
        o_ref[:, sl] = jnp.dot(p.astype(BF16), vw, preferred_element_type=F32) / den
        l_ref[:, sl] = jnp.broadcast_to(m + jnp.log(den), (t, DIL_HD))


def _dilated_group(qkv, g, seq):
    d, rows, _ = qkv.shape
    t, r = DIL_TILE, DIL_RADIUS
    w = DIL_HEADS * DIL_HD
    sub = t // r
    last = rows // r - 1

    def prev_blk(i):
        return jnp.maximum(i * sub - 1, 0)

    def next_blk(i):
        return jnp.minimum((i + 1) * sub, last)

    def cur(col):
        return pl.BlockSpec((None, t, w), lambda res, i: (res, i, col))

    def halo(col, blk):
        return pl.BlockSpec((None, r, w), lambda res, i: (res, blk(i), col))

    out_spec = pl.BlockSpec((None, t, w), lambda res, i: (res, i, 0))
    return pl.pallas_call(
        functools.partial(_dil_kernel, rows_per_seq=seq // d),
        out_shape=(jax.ShapeDtypeStruct((d, rows, w), F32),) * 2,
        grid=(d, rows // t),
        in_specs=[cur(0), halo(1, prev_blk), cur(1), halo(1, next_blk), halo(2, prev_blk), cur(2), halo(2, next_blk)],
        out_specs=(out_spec, out_spec),
        compiler_params=_params(("parallel", "arbitrary")),
        name=f"dilated_{g}",
    )(qkv, qkv, qkv, qkv, qkv, qkv, qkv)


def _dil_combine_kernel(o0, o1, o2, l0, l1, l2, y_ref, *stage):
    def token_order(ref, slot, h):
        d, rows, _ = ref.shape
        lanes = slice(h * DIL_HD, (h + 1) * DIL_HD)
        if d == 1:
            return ref[0, :, lanes]
        buf = stage[slot]
        for r in range(d):
            buf[h, pl.ds(r, rows, stride=d), :] = ref[r, :, lanes]
        return buf[h]

    for h in range(DIL_HEADS):
        a, b, c = token_order(l0, 0, h), token_order(l1, 0, h), token_order(l2, 1, h)
        mx = jnp.maximum(jnp.maximum(a, b), c)
        ea, eb, ec = jnp.exp(a - mx), jnp.exp(b - mx), jnp.exp(c - mx)
        inv = 1.0 / (ea + eb + ec)
        y = (ea * inv * token_order(o0, 2, h) + eb * inv * token_order(o1, 2, h)
             + ec * inv * token_order(o2, 3, h))
        y_ref[:, h * DIL_HD:(h + 1) * DIL_HD] = y.astype(y_ref.dtype)


def _dil_combine(outs, lses, tm):
    w = outs[0].shape[2]
    m = outs[0].shape[0] * outs[0].shape[1]
    specs = [pl.BlockSpec((a.shape[0], tm // a.shape[0], w), lambda i: (0, i, 0)) for a in (*outs, *lses)]
    return pl.pallas_call(
        _dil_combine_kernel,
        out_shape=jax.ShapeDtypeStruct((m, w), BF16),
        grid=(m // tm,),
        in_specs=specs,
        out_specs=pl.BlockSpec((tm, w), lambda i: (i, 0)),
        scratch_shapes=[pltpu.VMEM((DIL_HEADS, tm, DIL_HD), F32)] * 4,
        compiler_params=_params(("parallel",)),
        name="dilated_combine",
    )(*outs, *lses)


def _rope64(x, cos, sin_a, sin_b):
    return x * cos + pltpu.roll(x, 96, 1) * sin_a + pltpu.roll(x, 32, 1) * sin_b


def _mla_post_kernel(c_ref, gq_ref, gkv_ref, wuq_ref, wkn_ref, wv_ref, cos_ref, sa_ref, sb_ref,
                     q_ref, k_ref, v_ref):
    c = c_ref[...]
    cq = c[:, :MLA_QRANK]
    cq = cq * lax.rsqrt(jnp.mean(cq * cq, axis=-1, keepdims=True) + RMS_EPS) * gq_ref[...]
    ckv = c[:, MLA_QRANK:MLA_QRANK + MLA_KVRANK]
    ckv = (ckv * lax.rsqrt(jnp.mean(ckv * ckv, axis=-1, keepdims=True) + RMS_EPS) * gkv_ref[...]).astype(BF16)
    cos, sa, sb = cos_ref[...], sa_ref[...], sb_ref[...]
    scale = (MLA_NOPE + MLA_ROPE) ** -0.5
    q = jnp.dot(cq.astype(BF16), wuq_ref[...], preferred_element_type=F32) * scale
    kn = jnp.dot(ckv, wkn_ref[...], preferred_element_type=F32)
    kr = _rope64(c[:, MLA_QRANK + MLA_KVRANK:MLA_QRANK + MLA_KVRANK + LANE], cos, sa, sb).astype(BF16)
    for h in range(MLA_HEADS):
        base = h * MLA_HPAD
        q_ref[:, base:base + LANE] = q[:, base:base + LANE].astype(BF16)
        q_ref[:, base + LANE:base + 2 * LANE] = _rope64(q[:, base + LANE:base + 2 * LANE], cos, sa, sb).astype(BF16)
        k_ref[:, base:base + LANE] = kn[:, h * LANE:(h + 1) * LANE].astype(BF16)
        k_ref[:, base + LANE:base + 2 * LANE] = kr
    vv = jnp.dot(ckv, wv_ref[...], preferred_element_type=F32).astype(BF16)
    ones = jnp.ones((vv.shape[0], LANE), BF16)
    for h in range(MLA_HEADS):
        base = h * MLA_HPAD
        v_ref[:, base:base + LANE] = vv[:, h * LANE:(h + 1) * LANE]
        v_ref[:, base + LANE:base + 2 * LANE] = ones


def _mla_post(c, gq, gkv, wuq, wkn, wv, cos, sa, sb, seq, tm):
    m = c.shape[0]
    tiles_per_seq = seq // tm
    const = lambda i: (0, 0)
    tab = pl.BlockSpec((tm, LANE), lambda i: (i % tiles_per_seq, 0))
    hw = MLA_HEADS * MLA_HPAD
    return pl.pallas_call(
        _mla_post_kernel,
        out_shape=(jax.ShapeDtypeStruct((m, hw), BF16),) * 3,
        grid=(m // tm,),
        in_specs=[pl.BlockSpec((tm, C_COLS), lambda i: (i, 0)),
                  pl.BlockSpec((1, MLA_QRANK), const), pl.BlockSpec((1, MLA_KVRANK), const),
                  pl.BlockSpec(wuq.shape, const), pl.BlockSpec(wkn.shape, const), pl.BlockSpec(wv.shape, const),
                  tab, tab, tab],
        out_specs=(pl.BlockSpec((tm, hw), lambda i: (i, 0)),) * 3,
        compiler_params=_params(("parallel",)),
        name="mla_post",
    )(c, gq, gkv, wuq, wkn, wv, cos, sa, sb)


def _mla_attn_kernel(q_ref, k_ref, v_ref, o_ref, m_ref, acc_ref, s_ref, *, tk, sub):
    m_ref[...] = jnp.full_like(m_ref, -jnp.inf)
    acc_ref[...] = jnp.zeros_like(acc_ref)
    n_sub = q_ref.shape[0] // sub
    n_chunks = k_ref.shape[0] // tk
    nt = (((1,), (1,)), ((), ()))

    def chunk_rows(ci):
        return pl.ds(pl.multiple_of(ci * tk, tk), tk)

    def step(slot, ci):
        k_next = k_ref[chunk_rows(jnp.minimum(ci + 1, n_chunks - 1)), :]
        v_cur = v_ref[chunk_rows(ci), :]
        for r in range(n_sub):
            rs = slice(r * sub, (r + 1) * sub)
            s_ref[1 - slot, rs, :] = lax.dot_general(q_ref[rs, :], k_next, nt, preferred_element_type=F32)
            s = s_ref[slot, rs, :]
            m_prev = m_ref[rs, :]
            m_new = jnp.maximum(m_prev, jnp.max(s, axis=-1, keepdims=True))
            alpha = jnp.exp(m_prev - m_new)
            p = jnp.exp(s - jnp.concatenate([m_new] * (tk // LANE), axis=1))
            acc_ref[rs, :] = (jnp.concatenate([alpha, alpha], axis=1) * acc_ref[rs, :]
                              + jnp.dot(p.astype(BF16), v_cur, preferred_element_type=F32))
            m_ref[rs, :] = m_new

    def body(i, carry):
        step(0, 2 * i)
        step(1, 2 * i + 1)
        return carry

    k_first = k_ref[chunk_rows(0), :]
    for r in range(n_sub):
        rs = slice(r * sub, (r + 1) * sub)
        s_ref[0, rs, :] = lax.dot_general(q_ref[rs, :], k_first, nt, preferred_element_type=F32)
    lax.fori_loop(0, n_chunks // 2, body, 0)
    acc = acc_ref[...]
    o_ref[...] = (acc[:, :MLA_V] / acc[:, MLA_V:]).astype(o_ref.dtype)


def _mla_attn(q, k, v, seq, tq, tk, sub=256):
    m = q.shape[0]
    nseq = m // seq
    qt = seq // tq
    assert seq % (2 * tk) == 0 and tq % min(sub, tq) == 0
    return pl.pallas_call(
        functools.partial(_mla_attn_kernel, tk=tk, sub=min(sub, tq)),
        out_shape=jax.ShapeDtypeStruct((m, MLA_HEADS * MLA_V), BF16),
        grid=(nseq, MLA_HEADS, qt),
        in_specs=[pl.BlockSpec((tq, MLA_HPAD), lambda b, h, i: (b * qt + i, h)),
                  pl.BlockSpec((seq, MLA_HPAD), lambda b, h, i: (b, h)),
                  pl.BlockSpec((seq, MLA_HPAD), lambda b, h, i: (b, h))],
        out_specs=pl.BlockSpec((tq, MLA_V), lambda b, h, i: (b * qt + i, h)),
        scratch_shapes=[pltpu.VMEM((tq, LANE), F32), pltpu.VMEM((tq, MLA_HPAD), F32),
                        pltpu.VMEM((2, tq, tk), F32)],
        compiler_params=_params(("parallel", "parallel", "arbitrary")),
        name="mla_attn",
    )(q, k, v)


def _merge_kernel(x_ref, wg0, wg1, wg2, ya, yb, yc, wb0, wb1, wb2, o_ref):
    x = x_ref[...]
    acc = None
    for wg, y, wb in ((wg0, ya, wb0), (wg1, yb, wb1), (wg2, yc, wb2)):
        gate = _sigmoid(jnp.dot(x, wg[...], preferred_element_type=F32))
        term = gate * jnp.dot(y[...], wb[...], preferred_element_type=F32)
        acc = term if acc is None else acc + term
    o_ref[...] = acc.astype(o_ref.dtype)


def _merge(x, w_gate, ys, w_branch, tm, tn):
    m, k = x.shape
    n = w_branch.shape[2]
    nj = n // tn
    kb = w_branch.shape[1]
    gate_specs = [pl.BlockSpec((k, tn), functools.partial(lambda i, j, b: (0, b * nj + j), b=b)) for b in range(3)]
    y_specs = [pl.BlockSpec((tm, kb), lambda i, j: (i, 0))] * 3
    br_specs = [pl.BlockSpec((None, kb, tn), functools.partial(lambda i, j, b: (b, 0, j), b=b)) for b in range(3)]
    return pl.pallas_call(
        _merge_kernel,
        out_shape=jax.ShapeDtypeStruct((m, n), BF16),
        grid=(m // tm, nj),
        in_specs=[pl.BlockSpec((tm, k), lambda i, j: (i, 0))] + gate_specs + y_specs + br_specs,
        out_specs=pl.BlockSpec((tm, tn), lambda i, j: (i, j)),
        compiler_params=_params(("parallel", "arbitrary")),
        name="merge",
    )(x, w_gate, w_gate, w_gate, *ys, w_branch, w_branch, w_branch)


def _prep_layer(w_in, gate_w, gate_b, norm_g, q_norm_g, uq, kv_norm_g, ukv, w_branch, w_out,
                ffn_g, ffn_u, ffn_d, ln_g, ln_b):
    d = D_MODEL
    n_a = 2 * GLA_HEADS * GLA_DK + 2 * GLA_HEADS * GLA_DV
    n_rank = 2 * GLA_RANK
    n_b = len(DIL_DILATIONS) * 3 * BRANCH_W
    n_c = MLA_QRANK + MLA_KVRANK + MLA_ROPE
    o_b = n_a + n_rank
    o_c = o_b + n_b
    o_g = o_c + n_c
    w_c = jnp.concatenate([w_in[:, o_c:o_g], jnp.zeros((d, LANE - MLA_ROPE), F32),
                           w_in[:, n_a:o_b], jnp.zeros((d, LANE - n_rank), F32)], axis=1)
    gw = jnp.zeros((LANE, 2 * GLA_HEADS * GLA_DK), F32)
    gw = gw.at[:GLA_RANK, :GLA_HEADS * GLA_DK].set(gate_w[0])
    gw = gw.at[GLA_RANK:2 * GLA_RANK, GLA_HEADS * GLA_DK:].set(gate_w[1])
    uq_pad = jnp.pad(uq.reshape(MLA_QRANK, MLA_HEADS, MLA_NOPE + MLA_ROPE),
                     ((0, 0), (0, 0), (0, MLA_HPAD - MLA_NOPE - MLA_ROPE))).reshape(MLA_QRANK, MLA_HEADS * MLA_HPAD)
    ukv3 = ukv.reshape(MLA_KVRANK, MLA_HEADS, MLA_NOPE + MLA_V)
    fpad = D_FF_PAD - D_FF
    return dict(
        w_a=w_in[:, :n_a].astype(BF16),
        w_dil=[w_in[:, o_b + g * 3 * BRANCH_W:o_b + (g + 1) * 3 * BRANCH_W].astype(BF16)
               for g in range(len(DIL_DILATIONS))],
        w_c=w_c.astype(BF16),
        w_gate=w_in[:, o_g:].astype(BF16),
        gla_gw=gw.astype(BF16),
        gla_gb=jnp.concatenate([gate_b[0], gate_b[1]])[None, :],
        gla_norm=norm_g[None, :],
        q_norm=q_norm_g[None, :],
        kv_norm=kv_norm_g[None, :],
        w_uq=uq_pad.astype(BF16),
        w_kn=ukv3[:, :, :MLA_NOPE].reshape(MLA_KVRANK, -1).astype(BF16),
        w_vv=ukv3[:, :, MLA_NOPE:].reshape(MLA_KVRANK, -1).astype(BF16),
        w_branch=w_branch.astype(BF16),
        w_out=w_out.astype(BF16),
        ffn_g=jnp.pad(ffn_g, ((0, 0), (0, 0), (0, fpad))).astype(BF16),
        ffn_u=jnp.pad(ffn_u, ((0, 0), (0, 0), (0, fpad))).astype(BF16),
        ffn_d=jnp.pad(ffn_d, ((0, 0), (0, fpad), (0, 0))).astype(BF16),
        ln_g=ln_g[:, None, :],
        ln_b=ln_b[:, None, :],
    )


def _rope_tables(seq):
    pos = jnp.arange(seq, dtype=F32)[:, None]
    half = DIL_HD // 2
    ang = pos * jnp.power(ROPE_THETA, -2.0 * jnp.arange(half, dtype=F32) / DIL_HD)[None, :]
    cos128 = jnp.concatenate([jnp.cos(ang), jnp.cos(ang)], axis=1)
    sin128 = jnp.concatenate([-jnp.sin(ang), jnp.sin(ang)], axis=1)
    half = MLA_ROPE // 2
    ang = pos * jnp.power(ROPE_THETA, -2.0 * jnp.arange(half, dtype=F32) / MLA_ROPE)[None, :]
    zero = jnp.zeros((seq, half), F32)
    pad = jnp.zeros((seq, LANE - MLA_ROPE), F32)
    cos64 = jnp.concatenate([jnp.cos(ang), jnp.cos(ang), pad], axis=1)
    sin_a = jnp.concatenate([-jnp.sin(ang), zero, pad], axis=1)
    sin_b = jnp.concatenate([zero, jnp.sin(ang), pad], axis=1)
    return cos128, sin128, cos64, sin_a, sin_b


def _layer(x, xb, p, tabs, seq):
    cos128, sin128, cos64, sin_a, sin_b = tabs
    h = _ffn_up(xb, p["ffn_g"][0], p["ffn_u"][0], 1024, 256)
    x, xb = _mm_res_ln(h, p["ffn_d"][0], x, p["ln_g"][0], p["ln_b"][0], 0.5, 512, 512, "ffn_down_ln")

    pa = _proj(xb, p["w_a"], F32, 1024, 512, "proj_gla")
    c = _proj(xb, p["w_c"], F32, 512, C_COLS, "proj_c")

    lg = _gla_gates(c, p["gla_gw"], p["gla_gb"], 512)
    y_a = _gla(pa, lg, p["gla_norm"], seq)

    outs, lses = [], []
    for g, d in enumerate(DIL_DILATIONS):
        qkv = _proj_dil(xb, p["w_dil"][g], cos128, sin128, seq, d, 1024, 512, f"proj_dil_{g}")
        o, lse = _dilated_group(qkv, g, seq)
        outs.append(o)
        lses.append(lse)
    y_b = _dil_combine(outs, lses, 256)

    q_pad, k_pad, v_m = _mla_post(c, p["q_norm"], p["kv_norm"], p["w_uq"], p["w_kn"], p["w_vv"],
                                  cos64, sin_a, sin_b, seq, 256)
    y_c = _mla_attn(q_pad, k_pad, v_m, seq, 1024, 512)

    merged = _merge(xb, p["w_gate"], (y_a, y_b, y_c), p["w_branch"], 512, 256)
    x, xb = _mm_res_ln(merged, p["w_out"], x, p["ln_g"][1], p["ln_b"][1], 1.0, 512, 512, "out_proj_ln")

    h = _ffn_up(xb, p["ffn_g"][1], p["ffn_u"][1], 1024, 256)
    return _mm_res_ln(h, p["ffn_d"][1], x, p["ln_g"][2], p["ln_b"][2], 0.5, 512, 512, "ffn_down_ln")


def _trunk(x3, layers):
    bsz, seq, d = x3.shape
    x = x3.reshape(bsz * seq, d)
    xb = x.astype(BF16)
    tabs = _rope_tables(seq)
    for p in layers:
        x, xb = _layer(x, xb, p, tabs, seq)
    return x.reshape(bsz, seq, d)


def kernel(x_prompt, x_sample, w_in, gla_gate_w, gla_gate_b, gla_norm_g, mla_q_norm_g, mla_uq, mla_kv_norm_g,
           mla_ukv, w_branch, w_out, ffn_w_gate, ffn_w_up, ffn_w_down, ln_g, ln_b):
    layers = [_prep_layer(w_in[l], gla_gate_w[l], gla_gate_b[l], gla_norm_g[l], mla_q_norm_g[l], mla_uq[l],
                          mla_kv_norm_g[l], mla_ukv[l], w_branch[l], w_out[l], ffn_w_gate[l], ffn_w_up[l],
                          ffn_w_down[l], ln_g[l], ln_b[l]) for l in range(N_LAYERS)]
    return (_trunk(x_prompt, layers), _trunk(x_sample, layers))
```

```python
import functools
import math

import jax
import jax.numpy as jnp
from jax import lax
from jax.experimental import pallas as pl
from jax.experimental.pallas import tpu as pltpu

F32 = jnp.float32
BF16 = jnp.bfloat16

D_MODEL = 4096
N_LAYERS = 2
D_FF = 11008
D_FF_PAD = 11264
ROPE_THETA = 10000.0
LN_EPS = 1e-5
RMS_EPS = 1e-6
NEG_INF = -1e30
DEEPNORM_ALPHA = (2 * N_LAYERS) ** 0.25
BRANCH_W = 2048

GLA_HEADS = 4
GLA_DK = 256
GLA_DV = 512
GLA_RANK = 16
GLA_NORMALIZER = 16.0
GLA_CHUNK = 64
GLA_SUB = 16
GLA_BLOCK = 256

DIL_DILATIONS = (1, 4, 16)
DIL_RADIUS = 64
DIL_HEADS = 16
DIL_HD = 128
DIL_TILE = 256
PERM_ROWS = 256

MLA_HEADS = 16
MLA_NOPE = 128
MLA_ROPE = 64
MLA_V = 128
MLA_QRANK = 896
MLA_KVRANK = 256
MLA_HPAD = 256
MLA_CHUNKS_PER_TRIP = 4
C_COLS = 1408
C_GATE_BLOCK = 10

LANE = 128
VMEM_LIMIT = 56 * 1024 * 1024
VMEM_LIMIT_WIDE = 60000 * 1024


def _params(sem, vmem=VMEM_LIMIT):
    return pltpu.CompilerParams(dimension_semantics=sem, vmem_limit_bytes=vmem)


def _sigmoid(x):
    return 1.0 / (1.0 + jnp.exp(-x))


def _proj_kernel(x_ref, w_ref, o_ref):
    o_ref[...] = jnp.dot(x_ref[...], w_ref[...], preferred_element_type=F32).astype(o_ref.dtype)


def _proj(x, w, out_dtype, tm, tn, name):
    m, k = x.shape
    n = w.shape[1]
    return pl.pallas_call(
        _proj_kernel,
        out_shape=jax.ShapeDtypeStruct((m, n), out_dtype),
        grid=(m // tm, n // tn),
        in_specs=[pl.BlockSpec((tm, k), lambda i, j: (i, 0)),
                  pl.BlockSpec((k, tn), lambda i, j: (0, j))],
        out_specs=pl.BlockSpec((tm, tn), lambda i, j: (i, j)),
        compiler_params=_params(("parallel", "arbitrary")),
        name=name,
    )(x, w)


def _proj_dil_kernel(x_ref, w_ref, cos_ref, sin_ref, o_ref, *, d, n_q_tiles, n_rope_tiles, q_scale, sub_n):
    j = pl.program_id(1)
    x = x_ref[...]
    is_rope = j < n_rope_tiles
    scale = jnp.where(j < n_q_tiles, q_scale, 1.0).astype(F32)
    cos = jnp.where(is_rope, cos_ref[...] * scale, 1.0)
    sin = jnp.where(is_rope, sin_ref[...] * scale, 0.0)
    if d > 1:
        per = PERM_ROWS // d
        out_row = lax.broadcasted_iota(jnp.int32, (PERM_ROWS, PERM_ROWS), 0)
        in_row = lax.broadcasted_iota(jnp.int32, (PERM_ROWS, PERM_ROWS), 1)
        src = (out_row & (per - 1)) * d + lax.shift_right_logical(out_row, per.bit_length() - 1)
        perm = jnp.where(in_row == src, 1.0, 0.0).astype(BF16)
    for t in range(w_ref.shape[1] // sub_n):
        cols = slice(t * sub_n, (t + 1) * sub_n)
        acc = jnp.dot(x, w_ref[:, cols], preferred_element_type=F32)
        heads = []
        for h in range(sub_n // LANE):
            y = acc[:, h * LANE:(h + 1) * LANE]
            heads.append((y * cos + pltpu.roll(y, LANE // 2, 1) * sin).astype(BF16))
        res = jnp.concatenate(heads, axis=1)
        if d == 1:
            o_ref[0, :, cols] = res
            continue
        for b in range(res.shape[0] // PERM_ROWS):
            moved = jnp.dot(perm, res[b * PERM_ROWS:(b + 1) * PERM_ROWS, :],
                            preferred_element_type=F32).astype(o_ref.dtype)
            for r in range(d):
                o_ref[r, b * per:(b + 1) * per, cols] = moved[r * per:(r + 1) * per, :]


def _proj_dil(x, w, cos, sin, seq, d, tm, tn, name, sub_n=512):
    m, k = x.shape
    n = w.shape[1]
    tiles_per_seq = seq // tm
    w_head = DIL_HEADS * DIL_HD
    return pl.pallas_call(
        functools.partial(_proj_dil_kernel, d=d, n_q_tiles=w_head // tn, n_rope_tiles=2 * w_head // tn,
                          q_scale=DIL_HD ** -0.5, sub_n=min(sub_n, tn)),
        out_shape=jax.ShapeDtypeStruct((d, m // d, n), BF16),
        grid=(m // tm, n // tn),
        in_specs=[pl.BlockSpec((tm, k), lambda i, j: (i, 0)),
                  pl.BlockSpec((k, tn), lambda i, j: (0, j)),
                  pl.BlockSpec((tm, LANE), lambda i, j: (i % tiles_per_seq, 0)),
                  pl.BlockSpec((tm, LANE), lambda i, j: (i % tiles_per_seq, 0))],
        out_specs=pl.BlockSpec((d, tm // d, tn), lambda i, j: (0, i, j)),
        compiler_params=_params(("parallel", "arbitrary")),
        name=name,
    )(x, w, cos, sin)


def _ffn_up_kernel(x_ref, wg_ref, wu_ref, o_ref):
    x = x_ref[...]
    g = jnp.dot(x, wg_ref[...], preferred_element_type=F32)
    u = jnp.dot(x, wu_ref[...], preferred_element_type=F32)
    o_ref[...] = (g * _sigmoid(g) * u).astype(o_ref.dtype)


def _ffn_up(x, wg, wu, tm, tn):
    m, k = x.shape
    n = wg.shape[1]
    return pl.pallas_call(
        _ffn_up_kernel,
        out_shape=jax.ShapeDtypeStruct((m, n), BF16),
        grid=(m // tm, n // tn),
        in_specs=[pl.BlockSpec((tm, k), lambda i, j: (i, 0)),
                  pl.BlockSpec((k, tn), lambda i, j: (0, j)),
                  pl.BlockSpec((k, tn), lambda i, j: (0, j))],
        out_specs=pl.BlockSpec((tm, tn), lambda i, j: (i, j)),
        compiler_params=_params(("parallel", "arbitrary")),
        name="ffn_up",
    )(x, wg, wu)


def _mm_res_ln_kernel(a_ref, w_ref, x_ref, g_ref, b_ref, of_ref, ob_ref, *, scale):
    kk = pl.program_id(1)

    @pl.when(kk == 0)
    def _():
        of_ref[...] = jnp.zeros_like(of_ref)

    of_ref[...] += jnp.dot(a_ref[...], w_ref[...], preferred_element_type=F32)

    @pl.when(kk == pl.num_programs(1) - 1)
    def _():
        z = DEEPNORM_ALPHA * x_ref[...] + scale * of_ref[...]
        mu = jnp.mean(z, axis=-1, keepdims=True)
        zc = z - mu
        var = jnp.mean(zc * zc, axis=-1, keepdims=True)
        y = zc * lax.rsqrt(var + LN_EPS) * g_ref[...] + b_ref[...]
        of_ref[...] = y
        ob_ref[...] = y.astype(BF16)


def _mm_res_ln(a, w, x, g, b, scale, tm, tk, name):
    m, k = a.shape
    n = w.shape[1]
    return pl.pallas_call(
        functools.partial(_mm_res_ln_kernel, scale=scale),
        out_shape=(jax.ShapeDtypeStruct((m, n), F32), jax.ShapeDtypeStruct((m, n), BF16)),
        grid=(m // tm, k // tk),
        in_specs=[pl.BlockSpec((tm, tk), lambda i, kk: (i, kk)),
                  pl.BlockSpec((tk, n), lambda i, kk: (kk, 0)),
                  pl.BlockSpec((tm, n), lambda i, kk: (i, 0)),
                  pl.BlockSpec((1, n), lambda i, kk: (0, 0)),
                  pl.BlockSpec((1, n), lambda i, kk: (0, 0))],
        out_specs=(pl.BlockSpec((tm, n), lambda i, kk: (i, 0)),
                   pl.BlockSpec((tm, n), lambda i, kk: (i, 0))),
        compiler_params=_params(("parallel", "arbitrary"), VMEM_LIMIT_WIDE),
        name=name,
    )(a, w, x, g, b)


def _gla_gate_kernel(c_ref, gw_ref, gb_ref, o_ref):
    z = jnp.dot(c_ref[...].astype(BF16), gw_ref[...], preferred_element_type=F32) + gb_ref[...]
    log_sig = jnp.minimum(z, 0.0) - jnp.log(1.0 + jnp.exp(-jnp.abs(z)))
    o_ref[...] = log_sig * (1.0 / GLA_NORMALIZER)


def _gla_gates(c, gw, gb, tm):
    m = c.shape[0]
    n = gw.shape[1]
    return pl.pallas_call(
        _gla_gate_kernel,
        out_shape=jax.ShapeDtypeStruct((m, n), F32),
        grid=(m // tm,),
        in_specs=[pl.BlockSpec((tm, LANE), lambda i: (i, C_GATE_BLOCK)),
                  pl.BlockSpec((LANE, n), lambda i: (0, 0)),
                  pl.BlockSpec((1, n), lambda i: (0, 0))],
        out_specs=pl.BlockSpec((tm, n), lambda i: (i, 0)),
        compiler_params=_params(("parallel",)),
        name="gla_gates",
    )(c, gw, gb)


def _gla_chunk(q, k, v, lg, st_ref, reverse):
    c, sub = GLA_CHUNK, GLA_SUB
    nsub = c // sub
    row = lax.broadcasted_iota(jnp.int32, (c, c), 0)
    col = lax.broadcasted_iota(jnp.int32, (c, c), 1)
    tri = jnp.where((col >= row) if reverse else (col <= row), 1.0, 0.0).astype(BF16)
    hi = lg.astype(BF16)
    r1 = lg - hi.astype(F32)
    mid = r1.astype(BF16)
    lo = (r1 - mid.astype(F32)).astype(BF16)
    b = (jnp.dot(tri, hi, preferred_element_type=F32) + jnp.dot(tri, mid, preferred_element_type=F32)
         + jnp.dot(tri, lo, preferred_element_type=F32))
    b = b * math.log2(math.e)
    total = b[0:1, :] if reverse else b[c - 1:c, :]

    q = q * (GLA_DK ** -0.5)
    st = st_ref[...]
    o = lax.dot_general((q * jnp.exp2(b)).astype(BF16), st.astype(BF16), (((1,), (1,)), ((), ())),
                        preferred_element_type=F32)
    k_state = (k * jnp.exp2(total - b)).astype(BF16)
    st_ref[...] = st * jnp.exp2(total) + lax.dot_general(
        v.astype(BF16), k_state, (((0,), (0,)), ((), ())), preferred_element_type=F32)

    lane = lax.broadcasted_iota(jnp.int32, (sub, c), 1)
    lrow = lax.broadcasted_iota(jnp.int32, (sub, c), 0)
    blocks = []
    for i_sub in range(nsub):
        r0 = i_sub * sub
        q_r = q[r0:r0 + sub, :]
        b_r = b[r0:r0 + sub, :]
        a_blk = jnp.zeros((sub, c), F32)
        has_off = (i_sub < nsub - 1) if reverse else (i_sub > 0)
        if has_off:
            edge = r0 + sub if reverse else r0 - 1
            ref = b[edge:edge + 1, :]
            q_t = (q_r * jnp.exp2(b_r - ref)).astype(BF16)
            k_t = (k * jnp.exp2(jnp.minimum(ref - b, 0.0))).astype(BF16)
            a_off = lax.dot_general(q_t, k_t, (((1,), (1,)), ((), ())), preferred_element_type=F32)
            keep = (lane >= r0 + sub) if reverse else (lane < r0)
            a_blk = jnp.where(keep, a_off, 0.0)
        for jj in range(sub):
            j = r0 + jj
            w = q_r * k[j:j + 1, :] * jnp.exp2(jnp.minimum(b_r - b[j:j + 1, :], 0.0))
            s = jnp.sum(w, axis=1, keepdims=True)
            keep = (lrow < jj) if reverse else (lrow >= jj)
            a_blk = a_blk + jnp.where((lane == j) & keep, s, 0.0)
        blocks.append(a_blk)
    a = jnp.concatenate(blocks, axis=0)
    return o + jnp.dot(a.astype(BF16), v.astype(BF16), preferred_element_type=F32)


def _gla_fwd_kernel(q_ref, k_ref, v_ref, lg_ref, o_ref, st_ref, *, blocks_per_seq):
    @pl.when(pl.program_id(1) % blocks_per_seq == 0)
    def _():
        st_ref[...] = jnp.zeros_like(st_ref)

    def body(ci, carry):
        rows = pl.ds(pl.multiple_of(ci * GLA_CHUNK, GLA_CHUNK), GLA_CHUNK)
        o_ref[rows, :] = _gla_chunk(q_ref[rows, :], k_ref[rows, :], v_ref[rows, :], lg_ref[rows, :],
                                    st_ref, False)
        return carry

    lax.fori_loop(0, GLA_BLOCK // GLA_CHUNK, body, 0, unroll=2)


def _gla_bwd_kernel(q_ref, k_ref, v_ref, lg_ref, of_ref, r_ref, g_ref, y_ref, st_ref, *, blocks_per_seq):
    @pl.when(pl.program_id(1) % blocks_per_seq == 0)
    def _():
        st_ref[...] = jnp.zeros_like(st_ref)

    n_chunks = GLA_BLOCK // GLA_CHUNK

    def body(ci, carry):
        rows = pl.ds(pl.multiple_of((n_chunks - 1 - ci) * GLA_CHUNK, GLA_CHUNK), GLA_CHUNK)
        o = of_ref[rows, :] + _gla_chunk(q_ref[rows, :], k_ref[rows, :], v_ref[rows, :], lg_ref[rows, :],
                                         st_ref, True)
        o = o * lax.rsqrt(jnp.mean(o * o, axis=-1, keepdims=True) + RMS_EPS) * g_ref[...]
        r = r_ref[rows, :]
        y_ref[rows, :] = (o * (r * _sigmoid(r))).astype(y_ref.dtype)
        return carry

    lax.fori_loop(0, n_chunks, body, 0, unroll=2)


def _gla(pa, lg, norm_g, seq):
    m = pa.shape[0]
    nb = m // GLA_BLOCK
    bps = seq // GLA_BLOCK
    hk = GLA_HEADS
    blk = GLA_BLOCK

    def specs(row_of):
        return [pl.BlockSpec((blk, GLA_DK), lambda h, i: (row_of(i), h)),
                pl.BlockSpec((blk, GLA_DK), lambda h, i: (row_of(i), hk + h)),
                pl.BlockSpec((blk, GLA_DV), lambda h, i: (row_of(i), hk + h))]

    fwd_row = lambda i: i
    bwd_row = lambda i: nb - 1 - i
    scratch = [pltpu.VMEM((GLA_DV, GLA_DK), F32)]
    o_f = pl.pallas_call(
        functools.partial(_gla_fwd_kernel, blocks_per_seq=bps),
        out_shape=jax.ShapeDtypeStruct((m, GLA_HEADS * GLA_DV), F32),
        grid=(GLA_HEADS, nb),
        in_specs=specs(fwd_row) + [pl.BlockSpec((blk, GLA_DK), lambda h, i: (i, h))],
        out_specs=pl.BlockSpec((blk, GLA_DV), lambda h, i: (i, h)),
        scratch_shapes=scratch,
        compiler_params=_params(("parallel", "arbitrary")),
        name="gla_fwd",
    )(pa, pa, pa, lg)
    return pl.pallas_call(
        functools.partial(_gla_bwd_kernel, blocks_per_seq=bps),
        out_shape=jax.ShapeDtypeStruct((m, GLA_HEADS * GLA_DV), BF16),
        grid=(GLA_HEADS, nb),
        in_specs=specs(bwd_row) + [
            pl.BlockSpec((blk, GLA_DK), lambda h, i: (bwd_row(i), hk + h)),
            pl.BlockSpec((blk, GLA_DV), lambda h, i: (bwd_row(i), h)),
            pl.BlockSpec((blk, GLA_DV), lambda h, i: (bwd_row(i), 2 * hk + h)),
            pl.BlockSpec((1, GLA_DV), lambda h, i: (0, 0))],
        out_specs=pl.BlockSpec((blk, GLA_DV), lambda h, i: (bwd_row(i), h)),
        scratch_shapes=scratch,
        compiler_params=_params(("parallel", "arbitrary")),
        name="gla_bwd",
    )(pa, pa, pa, lg, o_f, pa, norm_g)


def _dil_kernel(q_ref, kp_ref, kc_ref, kn_ref, vp_ref, vc_ref, vn_ref, o_ref, l_ref, *, rows_per_seq):
    t, r = DIL_TILE, DIL_RADIUS
    start = pl.program_id(1) * t
    seq_lo = (start // rows_per_seq) * rows_per_seq
    row = lax.broadcasted_iota(jnp.int32, (t, t + 2 * r), 0)
    col = lax.broadcasted_iota(jnp.int32, (t, t + 2 * r), 1)
    rel = col - row
    pos = start - r + col
    valid = (rel >= 0) & (rel <= 2 * r) & (pos >= seq_lo) & (pos < seq_lo + rows_per_seq)
    head_lane = lax.broadcasted_iota(jnp.int32, (t, LANE), 1)
    lse = jnp.zeros((t, LANE), F32)
    for h in range(DIL_HEADS):
        sl = slice(h * DIL_HD, (h + 1) * DIL_HD)
        kw = jnp.concatenate([kp_ref[:, sl], kc_ref[:, sl], kn_ref[:, sl]], axis=0)
        vw = jnp.concatenate([vp_ref[:, sl], vc_ref[:, sl], vn_ref[:, sl]], axis=0)
        s = lax.dot_general(q_ref[:, sl], kw, (((1,), (1,)), ((), ())), preferred_element_type=F32)
        s = jnp.where(valid, s, NEG_INF)
        m = jnp.max(s, axis=-1, keepdims=True)
        p = jnp.exp(s - m)
        den = jnp.sum(p, axis=-1, keepdims=True)
        o_ref[:, sl] = jnp.dot(p.astype(BF16), vw, preferred_element_type=F32) / den
        lse = jnp.where(head_lane == h, m + jnp.log(den), lse)
    l_ref[...] = lse


def _dilated_group(qkv, g, seq):
    d, rows, _ = qkv.shape
    t, r = DIL_TILE, DIL_RADIUS
    w = DIL_HEADS * DIL_HD
    sub = t // r
    last = rows // r - 1

    def prev_blk(i):
        return jnp.maximum(i * sub - 1, 0)

    def next_blk(i):
        return jnp.minimum((i + 1) * sub, last)

    def cur(col):
        return pl.BlockSpec((None, t, w), lambda res, i: (res, i, col))

    def halo(col, blk):
        return pl.BlockSpec((None, r, w), lambda res, i: (res, blk(i), col))

    return pl.pallas_call(
        functools.partial(_dil_kernel, rows_per_seq=seq // d),
        out_shape=(jax.ShapeDtypeStruct((d, rows, w), F32), jax.ShapeDtypeStruct((d, rows, LANE), F32)),
        grid=(d, rows // t),
        in_specs=[cur(0), halo(1, prev_blk), cur(1), halo(1, next_blk), halo(2, prev_blk), cur(2), halo(2, next_blk)],
        out_specs=(pl.BlockSpec((None, t, w), lambda res, i: (res, i, 0)),
                   pl.BlockSpec((None, t, LANE), lambda res, i: (res, i, 0))),
        compiler_params=_params(("parallel", "arbitrary")),
        name=f"dilated_{g}",
    )(qkv, qkv, qkv, qkv, qkv, qkv, qkv)


def _dil_combine_kernel(o0, o1, o2, l0, l1, l2, y_ref, so1, so2, sl1, sl2):
    def token_order(ref, buf, lanes=slice(None)):
        d, rows, _ = ref.shape
        if d == 1:
            return ref[0, :, lanes]
        for r in range(d):
            buf[pl.ds(r, rows, stride=d), :] = ref[r, :, lanes]
        return buf[...]

    a, b, c = token_order(l0, None), token_order(l1, sl1), token_order(l2, sl2)
    mx = jnp.maximum(jnp.maximum(a, b), c)
    ea, eb, ec = jnp.exp(a - mx), jnp.exp(b - mx), jnp.exp(c - mx)
    inv = 1.0 / (ea + eb + ec)
    wa, wb, wc = ea * inv, eb * inv, ec * inv
    for h in range(DIL_HEADS):
        lanes = slice(h * DIL_HD, (h + 1) * DIL_HD)
        y = (wa[:, h:h + 1] * token_order(o0, None, lanes) + wb[:, h:h + 1] * token_order(o1, so1.at[h], lanes)
             + wc[:, h:h + 1] * token_order(o2, so2.at[h], lanes))
        y_ref[:, lanes] = y.astype(y_ref.dtype)


def _dil_combine(outs, lses, tm):
    w = outs[0].shape[2]
    m = outs[0].shape[0] * outs[0].shape[1]
    specs = [pl.BlockSpec((a.shape[0], tm // a.shape[0], a.shape[2]), lambda i: (0, i, 0)) for a in (*outs, *lses)]
    return pl.pallas_call(
        _dil_combine_kernel,
        out_shape=jax.ShapeDtypeStruct((m, w), BF16),
        grid=(m // tm,),
        in_specs=specs,
        out_specs=pl.BlockSpec((tm, w), lambda i: (i, 0)),
        scratch_shapes=[pltpu.VMEM((DIL_HEADS, tm, DIL_HD), F32)] * 2 + [pltpu.VMEM((tm, LANE), F32)] * 2,
        compiler_params=_params(("parallel",)),
        name="dilated_combine",
    )(*outs, *lses)


def _rope64(x, cos, sin_a, sin_b):
    return x * cos + pltpu.roll(x, 96, 1) * sin_a + pltpu.roll(x, 32, 1) * sin_b


def _mla_post_kernel(c_ref, gq_ref, gkv_ref, wuq_ref, wkn_ref, wv_ref, cos_ref, sa_ref, sb_ref,
                     q_ref, k_ref, v_ref):
    c = c_ref[...]
    cq = c[:, :MLA_QRANK]
    cq = cq * lax.rsqrt(jnp.mean(cq * cq, axis=-1, keepdims=True) + RMS_EPS) * gq_ref[...]
    ckv = c[:, MLA_QRANK:MLA_QRANK + MLA_KVRANK]
    ckv = (ckv * lax.rsqrt(jnp.mean(ckv * ckv, axis=-1, keepdims=True) + RMS_EPS) * gkv_ref[...]).astype(BF16)
    cos, sa, sb = cos_ref[...], sa_ref[...], sb_ref[...]
    scale = (MLA_NOPE + MLA_ROPE) ** -0.5
    q = jnp.dot(cq.astype(BF16), wuq_ref[...], preferred_element_type=F32) * scale
    kn = jnp.dot(ckv, wkn_ref[...], preferred_element_type=F32)
    kr = _rope64(c[:, MLA_QRANK + MLA_KVRANK:MLA_QRANK + MLA_KVRANK + LANE], cos, sa, sb).astype(BF16)
    for h in range(MLA_HEADS):
        base = h * MLA_HPAD
        q_ref[:, base:base + LANE] = q[:, base:base + LANE].astype(BF16)
        q_ref[:, base + LANE:base + 2 * LANE] = _rope64(q[:, base + LANE:base + 2 * LANE], cos, sa, sb).astype(BF16)
        k_ref[:, base:base + LANE] = kn[:, h * LANE:(h + 1) * LANE].astype(BF16)
        k_ref[:, base + LANE:base + 2 * LANE] = kr
    vv = jnp.dot(ckv, wv_ref[...], preferred_element_type=F32).astype(BF16)
    ones = jnp.ones((vv.shape[0], LANE), BF16)
    for h in range(MLA_HEADS):
        base = h * MLA_HPAD
        v_ref[:, base:base + LANE] = vv[:, h * LANE:(h + 1) * LANE]
        v_ref[:, base + LANE:base + 2 * LANE] = ones


def _mla_post(c, gq, gkv, wuq, wkn, wv, cos, sa, sb, seq, tm):
    m = c.shape[0]
    tiles_per_seq = seq // tm
    const = lambda i: (0, 0)
    tab = pl.BlockSpec((tm, LANE), lambda i: (i % tiles_per_seq, 0))
    hw = MLA_HEADS * MLA_HPAD
    return pl.pallas_call(
        _mla_post_kernel,
        out_shape=(jax.ShapeDtypeStruct((m, hw), BF16),) * 3,
        grid=(m // tm,),
        in_specs=[pl.BlockSpec((tm, C_COLS), lambda i: (i, 0)),
                  pl.BlockSpec((1, MLA_QRANK), const), pl.BlockSpec((1, MLA_KVRANK), const),
                  pl.BlockSpec(wuq.shape, const), pl.BlockSpec(wkn.shape, const), pl.BlockSpec(wv.shape, const),
                  tab, tab, tab],
        out_specs=(pl.BlockSpec((tm, hw), lambda i: (i, 0)),) * 3,
        compiler_params=_params(("parallel",)),
        name="mla_post",
    )(c, gq, gkv, wuq, wkn, wv, cos, sa, sb)


def _mla_attn_kernel(q_ref, k_ref, v_ref, o_ref, m_ref, acc_ref, s_ref, *, tk, sub):
    m_ref[...] = jnp.full_like(m_ref, -jnp.inf)
    acc_ref[...] = jnp.zeros_like(acc_ref)
    n_sub = q_ref.shape[0] // sub
    n_chunks = k_ref.shape[0] // tk
    nt = (((1,), (1,)), ((), ()))

    def chunk_rows(ci):
        return pl.ds(pl.multiple_of(ci * tk, tk), tk)

    def step(slot, ci):
        k_next = k_ref[chunk_rows(jnp.minimum(ci + 1, n_chunks - 1)), :]
        v_cur = v_ref[chunk_rows(ci), :]
        for r in range(n_sub):
            rs = slice(r * sub, (r + 1) * sub)
            s_ref[1 - slot, rs, :] = lax.dot_general(q_ref[rs, :], k_next, nt, preferred_element_type=F32)
            s = s_ref[slot, rs, :]
            m_prev = m_ref[rs, :]
            m_new = jnp.maximum(m_prev, jnp.max(s, axis=-1, keepdims=True))
            alpha = jnp.exp(m_prev - m_new)
            p = jnp.exp(s - jnp.concatenate([m_new] * (tk // LANE), axis=1))
            acc_ref[rs, :] = (jnp.concatenate([alpha, alpha], axis=1) * acc_ref[rs, :]
                              + jnp.dot(p.astype(BF16), v_cur, preferred_element_type=F32))
            m_ref[rs, :] = m_new

    def body(i, carry):
        for c in range(MLA_CHUNKS_PER_TRIP):
            step(c % 2, MLA_CHUNKS_PER_TRIP * i + c)
        return carry

    k_first = k_ref[chunk_rows(0), :]
    for r in range(n_sub):
        rs = slice(r * sub, (r + 1) * sub)
        s_ref[0, rs, :] = lax.dot_general(q_ref[rs, :], k_first, nt, preferred_element_type=F32)
    lax.fori_loop(0, n_chunks // MLA_CHUNKS_PER_TRIP, body, 0)
    acc = acc_ref[...]
    o_ref[...] = (acc[:, :MLA_V] / acc[:, MLA_V:]).astype(o_ref.dtype)


def _mla_attn(q, k, v, seq, tq, tk, sub=256):
    m = q.shape[0]
    nseq = m // seq
    qt = seq // tq
    assert seq % (MLA_CHUNKS_PER_TRIP * tk) == 0 and tq % min(sub, tq) == 0
    return pl.pallas_call(
        functools.partial(_mla_attn_kernel, tk=tk, sub=min(sub, tq)),
        out_shape=jax.ShapeDtypeStruct((m, MLA_HEADS * MLA_V), BF16),
        grid=(nseq, MLA_HEADS, qt),
        in_specs=[pl.BlockSpec((tq, MLA_HPAD), lambda b, h, i: (b * qt + i, h)),
                  pl.BlockSpec((seq, MLA_HPAD), lambda b, h, i: (b, h)),
                  pl.BlockSpec((seq, MLA_HPAD), lambda b, h, i: (b, h))],
        out_specs=pl.BlockSpec((tq, MLA_V), lambda b, h, i: (b * qt + i, h)),
        scratch_shapes=[pltpu.VMEM((tq, LANE), F32), pltpu.VMEM((tq, MLA_HPAD), F32),
                        pltpu.VMEM((2, tq, tk), F32)],
        compiler_params=_params(("parallel", "parallel", "arbitrary")),
        name="mla_attn",
    )(q, k, v)


def _merge_kernel(x_ref, wg0, wg1, wg2, ya, yb, yc, wb0, wb1, wb2, o_ref):
    x = x_ref[...]
    acc = None
    for wg, y, wb in ((wg0, ya, wb0), (wg1, yb, wb1), (wg2, yc, wb2)):
        gate = _sigmoid(jnp.dot(x, wg[...], preferred_element_type=F32))
        term = gate * jnp.dot(y[...], wb[...], preferred_element_type=F32)
        acc = term if acc is None else acc + term
    o_ref[...] = acc.astype(o_ref.dtype)


def _merge(x, w_gate, ys, w_branch, tm, tn):
    m, k = x.shape
    n = w_branch.shape[2]
    nj = n // tn
    kb = w_branch.shape[1]
    gate_specs = [pl.BlockSpec((k, tn), functools.partial(lambda i, j, b: (0, b * nj + j), b=b)) for b in range(3)]
    y_specs = [pl.BlockSpec((tm, kb), lambda i, j: (i, 0))] * 3
    br_specs = [pl.BlockSpec((None, kb, tn), functools.partial(lambda i, j, b: (b, 0, j), b=b)) for b in range(3)]
    return pl.pallas_call(
        _merge_kernel,
        out_shape=jax.ShapeDtypeStruct((m, n), BF16),
        grid=(m // tm, nj),
        in_specs=[pl.BlockSpec((tm, k), lambda i, j: (i, 0))] + gate_specs + y_specs + br_specs,
        out_specs=pl.BlockSpec((tm, tn), lambda i, j: (i, j)),
        compiler_params=_params(("parallel", "arbitrary")),
        name="merge",
    )(x, w_gate, w_gate, w_gate, *ys, w_branch, w_branch, w_branch)


def _prep_layer(w_in, gate_w, gate_b, norm_g, q_norm_g, uq, kv_norm_g, ukv, w_branch, w_out,
                ffn_g, ffn_u, ffn_d, ln_g, ln_b):
    d = D_MODEL
    n_a = 2 * GLA_HEADS * GLA_DK + 2 * GLA_HEADS * GLA_DV
    n_rank = 2 * GLA_RANK
    n_b = len(DIL_DILATIONS) * 3 * BRANCH_W
    n_c = MLA_QRANK + MLA_KVRANK + MLA_ROPE
    o_b = n_a + n_rank
    o_c = o_b + n_b
    o_g = o_c + n_c
    w_c = jnp.concatenate([w_in[:, o_c:o_g], jnp.zeros((d, LANE - MLA_ROPE), F32),
                           w_in[:, n_a:o_b], jnp.zeros((d, LANE - n_rank), F32)], axis=1)
    gw = jnp.zeros((LANE, 2 * GLA_HEADS * GLA_DK), F32)
    gw = gw.at[:GLA_RANK, :GLA_HEADS * GLA_DK].set(gate_w[0])
    gw = gw.at[GLA_RANK:2 * GLA_RANK, GLA_HEADS * GLA_DK:].set(gate_w[1])
    uq_pad = jnp.pad(uq.reshape(MLA_QRANK, MLA_HEADS, MLA_NOPE + MLA_ROPE),
                     ((0, 0), (0, 0), (0, MLA_HPAD - MLA_NOPE - MLA_ROPE))).reshape(MLA_QRANK, MLA_HEADS * MLA_HPAD)
    ukv3 = ukv.reshape(MLA_KVRANK, MLA_HEADS, MLA_NOPE + MLA_V)
    fpad = D_FF_PAD - D_FF
    return dict(
        w_a=w_in[:, :n_a].astype(BF16),
        w_dil=[w_in[:, o_b + g * 3 * BRANCH_W:o_b + (g + 1) * 3 * BRANCH_W].astype(BF16)
               for g in range(len(DIL_DILATIONS))],
        w_c=w_c.astype(BF16),
        w_gate=w_in[:, o_g:].astype(BF16),
        gla_gw=gw.astype(BF16),
        gla_gb=jnp.concatenate([gate_b[0], gate_b[1]])[None, :],
        gla_norm=norm_g[None, :],
        q_norm=q_norm_g[None, :],
        kv_norm=kv_norm_g[None, :],
        w_uq=uq_pad.astype(BF16),
        w_kn=ukv3[:, :, :MLA_NOPE].reshape(MLA_KVRANK, -1).astype(BF16),
        w_vv=ukv3[:, :, MLA_NOPE:].reshape(MLA_KVRANK, -1).astype(BF16),
        w_branch=w_branch.astype(BF16),
        w_out=w_out.astype(BF16),
        ffn_g=jnp.pad(ffn_g, ((0, 0), (0, 0), (0, fpad))).astype(BF16),
        ffn_u=jnp.pad(ffn_u, ((0, 0), (0, 0), (0, fpad))).astype(BF16),
        ffn_d=jnp.pad(ffn_d, ((0, 0), (0, fpad), (0, 0))).astype(BF16),
        ln_g=ln_g[:, None, :],
        ln_b=ln_b[:, None, :],
    )


def _rope_tables(seq):
    pos = jnp.arange(seq, dtype=F32)[:, None]
    half = DIL_HD // 2
    ang = pos * jnp.power(ROPE_THETA, -2.0 * jnp.arange(half, dtype=F32) / DIL_HD)[None, :]
    cos128 = jnp.concatenate([jnp.cos(ang), jnp.cos(ang)], axis=1)
    sin128 = jnp.concatenate([-jnp.sin(ang), jnp.sin(ang)], axis=1)
    half = MLA_ROPE // 2
    ang = pos * jnp.power(ROPE_THETA, -2.0 * jnp.arange(half, dtype=F32) / MLA_ROPE)[None, :]
    zero = jnp.zeros((seq, half), F32)
    pad = jnp.zeros((seq, LANE - MLA_ROPE), F32)
    cos64 = jnp.concatenate([jnp.cos(ang), jnp.cos(ang), pad], axis=1)
    sin_a = jnp.concatenate([-jnp.sin(ang), zero, pad], axis=1)
    sin_b = jnp.concatenate([zero, jnp.sin(ang), pad], axis=1)
    return cos128, sin128, cos64, sin_a, sin_b


def _layer(x, xb, p, tabs, seq):
    cos128, sin128, cos64, sin_a, sin_b = tabs
    h = _ffn_up(xb, p["ffn_g"][0], p["ffn_u"][0], 1024, 256)
    x, xb = _mm_res_ln(h, p["ffn_d"][0], x, p["ln_g"][0], p["ln_b"][0], 0.5, 512, 512, "ffn_down_ln")

    pa = _proj(xb, p["w_a"], F32, 1024, 512, "proj_gla")
    c = _proj(xb, p["w_c"], F32, 512, C_COLS, "proj_c")

    lg = _gla_gates(c, p["gla_gw"], p["gla_gb"], 512)
    y_a = _gla(pa, lg, p["gla_norm"], seq)

    outs, lses = [], []
    for g, d in enumerate(DIL_DILATIONS):
        qkv = _proj_dil(xb, p["w_dil"][g], cos128, sin128, seq, d, 1024, 1024, f"proj_dil_{g}")
        o, lse = _dilated_group(qkv, g, seq)
        outs.append(o)
        lses.append(lse)
    y_b = _dil_combine(outs, lses, 256)

    q_pad, k_pad, v_m = _mla_post(c, p["q_norm"], p["kv_norm"], p["w_uq"], p["w_kn"], p["w_vv"],
                                  cos64, sin_a, sin_b, seq, 256)
    y_c = _mla_attn(q_pad, k_pad, v_m, seq, 1024, 512)

    merged = _merge(xb, p["w_gate"], (y_a, y_b, y_c), p["w_branch"], 512, 256)
    x, xb = _mm_res_ln(merged, p["w_out"], x, p["ln_g"][1], p["ln_b"][1], 1.0, 512, 512, "out_proj_ln")

    h = _ffn_up(xb, p["ffn_g"][1], p["ffn_u"][1], 1024, 256)
    return _mm_res_ln(h, p["ffn_d"][1], x, p["ln_g"][2], p["ln_b"][2], 0.5, 512, 512, "ffn_down_ln")


def _trunk(x3, layers):
    bsz, seq, d = x3.shape
    x = x3.reshape(bsz * seq, d)
    xb = x.astype(BF16)
    tabs = _rope_tables(seq)
    for p in layers:
        x, xb = _layer(x, xb, p, tabs, seq)
    return x.reshape(bsz, seq, d)


def kernel(x_prompt, x_sample, w_in, gla_gate_w, gla_gate_b, gla_norm_g, mla_q_norm_g, mla_uq, mla_kv_norm_g,
           mla_ukv, w_branch, w_out, ffn_w_gate, ffn_w_up, ffn_w_down, ln_g, ln_b):
    layers = [_prep_layer(w_in[l], gla_gate_w[l], gla_gate_b[l], gla_norm_g[l], mla_q_norm_g[l], mla_uq[l],
                          mla_kv_norm_g[l], mla_ukv[l], w_branch[l], w_out[l], ffn_w_gate[l], ffn_w_up[l],
                          ffn_w_down[l], ln_g[l], ln_b[l]) for l in range(N_LAYERS)]
    return (_trunk(x_prompt, layers), _trunk(x_sample, layers))
```

```python
import functools
import math

import jax
import jax.numpy as jnp
from jax import lax
from jax.experimental import pallas as pl
from jax.experimental.pallas import tpu as pltpu

F32 = jnp.float32
BF16 = jnp.bfloat16

D_MODEL = 4096
N_LAYERS = 2
D_FF = 11008
D_FF_PAD = 11264
FFN_SUB_N = 256
ROPE_THETA = 10000.0
LN_EPS = 1e-5
RMS_EPS = 1e-6
NEG_INF = -1e30
DEEPNORM_ALPHA = (2 * N_LAYERS) ** 0.25
BRANCH_W = 2048

GLA_HEADS = 4
GLA_DK = 256
GLA_DV = 512
GLA_RANK = 16
GLA_NORMALIZER = 16.0
GLA_CHUNK = 64
GLA_SUB = 16
GLA_BLOCK = 256

DIL_DILATIONS = (1, 4, 16)
DIL_RADIUS = 64
DIL_HEADS = 16
DIL_HD = 128
DIL_TILE = 256
DIL_QSUB = 128
PERM_ROWS = 256

MLA_HEADS = 16
MLA_NOPE = 128
MLA_ROPE = 64
MLA_V = 128
MLA_QRANK = 896
MLA_KVRANK = 256
MLA_HPAD = 256
MLA_CHUNKS_PER_TRIP = 4
C_COLS = 1408
C_GATE_BLOCK = 10

LANE = 128
VMEM_LIMIT = 56 * 1024 * 1024
VMEM_LIMIT_WIDE = 60000 * 1024


def _params(sem, vmem=VMEM_LIMIT):
    return pltpu.CompilerParams(dimension_semantics=sem, vmem_limit_bytes=vmem)


def _sigmoid(x):
    return 1.0 / (1.0 + jnp.exp(-x))


def _proj_kernel(x_ref, w_ref, o_ref):
    o_ref[...] = jnp.dot(x_ref[...], w_ref[...], preferred_element_type=F32).astype(o_ref.dtype)


def _proj(x, w, out_dtype, tm, tn, name):
    m, k = x.shape
    n = w.shape[1]
    return pl.pallas_call(
        _proj_kernel,
        out_shape=jax.ShapeDtypeStruct((m, n), out_dtype),
        grid=(m // tm, n // tn),
        in_specs=[pl.BlockSpec((tm, k), lambda i, j: (i, 0)),
                  pl.BlockSpec((k, tn), lambda i, j: (0, j))],
        out_specs=pl.BlockSpec((tm, tn), lambda i, j: (i, j)),
        compiler_params=_params(("parallel", "arbitrary")),
        name=name,
    )(x, w)


def _proj_dil_kernel(x_ref, w_ref, cos_ref, sin_ref, o_ref, *, d, n_q_tiles, n_rope_tiles, q_scale, sub_n):
    j = pl.program_id(1)
    x = x_ref[...]
    is_rope = j < n_rope_tiles
    scale = jnp.where(j < n_q_tiles, q_scale, 1.0).astype(F32)
    cos = jnp.where(is_rope, cos_ref[...] * scale, 1.0)
    sin = jnp.where(is_rope, sin_ref[...] * scale, 0.0)
    if d > 1:
        per = PERM_ROWS // d
        out_row = lax.broadcasted_iota(jnp.int32, (PERM_ROWS, PERM_ROWS), 0)
        in_row = lax.broadcasted_iota(jnp.int32, (PERM_ROWS, PERM_ROWS), 1)
        src = (out_row & (per - 1)) * d + lax.shift_right_logical(out_row, per.bit_length() - 1)
        perm = jnp.where(in_row == src, 1.0, 0.0).astype(BF16)
    for t in range(w_ref.shape[1] // sub_n):
        cols = slice(t * sub_n, (t + 1) * sub_n)
        acc = jnp.dot(x, w_ref[:, cols], preferred_element_type=F32)
        heads = []
        for h in range(sub_n // LANE):
            y = acc[:, h * LANE:(h + 1) * LANE]
            heads.append((y * cos + pltpu.roll(y, LANE // 2, 1) * sin).astype(BF16))
        res = jnp.concatenate(heads, axis=1)
        if d == 1:
            o_ref[0, :, cols] = res
            continue
        for b in range(res.shape[0] // PERM_ROWS):
            moved = jnp.dot(perm, res[b * PERM_ROWS:(b + 1) * PERM_ROWS, :],
                            preferred_element_type=F32).astype(o_ref.dtype)
            for r in range(d):
                o_ref[r, b * per:(b + 1) * per, cols] = moved[r * per:(r + 1) * per, :]


def _proj_dil(x, w, cos, sin, seq, d, tm, tn, name, sub_n=512):
    m, k = x.shape
    n = w.shape[1]
    tiles_per_seq = seq // tm
    w_head = DIL_HEADS * DIL_HD
    return pl.pallas_call(
        functools.partial(_proj_dil_kernel, d=d, n_q_tiles=w_head // tn, n_rope_tiles=2 * w_head // tn,
                          q_scale=DIL_HD ** -0.5, sub_n=min(sub_n, tn)),
        out_shape=jax.ShapeDtypeStruct((d, m // d, n), BF16),
        grid=(m // tm, n // tn),
        in_specs=[pl.BlockSpec((tm, k), lambda i, j: (i, 0)),
                  pl.BlockSpec((k, tn), lambda i, j: (0, j)),
                  pl.BlockSpec((tm, LANE), lambda i, j: (i % tiles_per_seq, 0)),
                  pl.BlockSpec((tm, LANE), lambda i, j: (i % tiles_per_seq, 0))],
        out_specs=pl.BlockSpec((d, tm // d, tn), lambda i, j: (0, i, j)),
        compiler_params=_params(("parallel", "arbitrary")),
        name=name,
    )(x, w, cos, sin)


def _ffn_up_kernel(x_ref, wg_ref, wu_ref, o_ref):
    x = x_ref[...]
    for t in range(o_ref.shape[1] // FFN_SUB_N):
        cols = slice(t * FFN_SUB_N, (t + 1) * FFN_SUB_N)
        g = jnp.dot(x, wg_ref[:, cols], preferred_element_type=F32)
        u = jnp.dot(x, wu_ref[:, cols], preferred_element_type=F32)
        o_ref[:, cols] = (g * _sigmoid(g) * u).astype(o_ref.dtype)


def _stacked_spec(lead, block, index_map):
    return pl.BlockSpec((None,) * len(lead) + tuple(block), lambda *g: tuple(lead) + tuple(index_map(*g)))


def _ffn_up(x, wg, wu, lead, tm, tn):
    m, k = x.shape
    n = wg.shape[-1]
    w_spec = _stacked_spec(lead, (k, tn), lambda i, j: (0, j))
    return pl.pallas_call(
        _ffn_up_kernel,
        out_shape=jax.ShapeDtypeStruct((m, n), BF16),
        grid=(m // tm, n // tn),
        in_specs=[pl.BlockSpec((tm, k), lambda i, j: (i, 0)), w_spec, w_spec],
        out_specs=pl.BlockSpec((tm, tn), lambda i, j: (i, j)),
        compiler_params=_params(("parallel", "arbitrary")),
        name="ffn_up",
    )(x, wg, wu)


def _mm_res_ln_kernel(a_ref, w_ref, x_ref, g_ref, b_ref, of_ref, ob_ref, *, scale):
    kk = pl.program_id(1)

    @pl.when(kk == 0)
    def _():
        of_ref[...] = jnp.zeros_like(of_ref)

    of_ref[...] += jnp.dot(a_ref[...], w_ref[...], preferred_element_type=F32)

    @pl.when(kk == pl.num_programs(1) - 1)
    def _():
        z = DEEPNORM_ALPHA * x_ref[...] + scale * of_ref[...]
        mu = jnp.mean(z, axis=-1, keepdims=True)
        zc = z - mu
        var = jnp.mean(zc * zc, axis=-1, keepdims=True)
        y = zc * lax.rsqrt(var + LN_EPS) * g_ref[...] + b_ref[...]
        of_ref[...] = y
        ob_ref[...] = y.astype(BF16)


def _mm_res_ln(a, w, lead, x, g, b, scale, tm, tk, name):
    m, k = a.shape
    n = w.shape[-1]
    return pl.pallas_call(
        functools.partial(_mm_res_ln_kernel, scale=scale),
        out_shape=(jax.ShapeDtypeStruct((m, n), F32), jax.ShapeDtypeStruct((m, n), BF16)),
        grid=(m // tm, k // tk),
        in_specs=[pl.BlockSpec((tm, tk), lambda i, kk: (i, kk)),
                  _stacked_spec(lead, (tk, n), lambda i, kk: (kk, 0)),
                  pl.BlockSpec((tm, n), lambda i, kk: (i, 0)),
                  pl.BlockSpec((1, n), lambda i, kk: (0, 0)),
                  pl.BlockSpec((1, n), lambda i, kk: (0, 0))],
        out_specs=(pl.BlockSpec((tm, n), lambda i, kk: (i, 0)),
                   pl.BlockSpec((tm, n), lambda i, kk: (i, 0))),
        compiler_params=_params(("parallel", "arbitrary"), VMEM_LIMIT_WIDE),
        name=name,
    )(a, w, x, g, b)


def _gla_gate_kernel(c_ref, gw_ref, gb_ref, o_ref):
    z = jnp.dot(c_ref[...].astype(BF16), gw_ref[...], preferred_element_type=F32) + gb_ref[...]
    log_sig = jnp.minimum(z, 0.0) - jnp.log(1.0 + jnp.exp(-jnp.abs(z)))
    o_ref[...] = log_sig * (1.0 / GLA_NORMALIZER)


def _gla_gates(c, gw, gb, tm):
    m = c.shape[0]
    n = gw.shape[1]
    return pl.pallas_call(
        _gla_gate_kernel,
        out_shape=jax.ShapeDtypeStruct((m, n), F32),
        grid=(m // tm,),
        in_specs=[pl.BlockSpec((tm, LANE), lambda i: (i, C_GATE_BLOCK)),
                  pl.BlockSpec((LANE, n), lambda i: (0, 0)),
                  pl.BlockSpec((1, n), lambda i: (0, 0))],
        out_specs=pl.BlockSpec((tm, n), lambda i: (i, 0)),
        compiler_params=_params(("parallel",)),
        name="gla_gates",
    )(c, gw, gb)


def _gla_chunk(q, k, v, lg, st_ref, reverse):
    c, sub = GLA_CHUNK, GLA_SUB
    nsub = c // sub
    row = lax.broadcasted_iota(jnp.int32, (c, c), 0)
    col = lax.broadcasted_iota(jnp.int32, (c, c), 1)
    tri = jnp.where((col >= row) if reverse else (col <= row), 1.0, 0.0).astype(BF16)
    hi = lg.astype(BF16)
    r1 = lg - hi.astype(F32)
    mid = r1.astype(BF16)
    lo = (r1 - mid.astype(F32)).astype(BF16)
    b = (jnp.dot(tri, hi, preferred_element_type=F32) + jnp.dot(tri, mid, preferred_element_type=F32)
         + jnp.dot(tri, lo, preferred_element_type=F32))
    b = b * math.log2(math.e)
    total = b[0:1, :] if reverse else b[c - 1:c, :]

    q = q * (GLA_DK ** -0.5)
    st = st_ref[...]
    o = lax.dot_general((q * jnp.exp2(b)).astype(BF16), st.astype(BF16), (((1,), (1,)), ((), ())),
                        preferred_element_type=F32)
    k_state = (k * jnp.exp2(total - b)).astype(BF16)
    st_ref[...] = st * jnp.exp2(total) + lax.dot_general(
        v.astype(BF16), k_state, (((0,), (0,)), ((), ())), preferred_element_type=F32)

    lane = lax.broadcasted_iota(jnp.int32, (sub, c), 1)
    lrow = lax.broadcasted_iota(jnp.int32, (sub, c), 0)
    blocks = []
    for i_sub in range(nsub):
        r0 = i_sub * sub
        q_r = q[r0:r0 + sub, :]
        b_r = b[r0:r0 + sub, :]
        a_blk = jnp.zeros((sub, c), F32)
        has_off = (i_sub < nsub - 1) if reverse else (i_sub > 0)
        if has_off:
            edge = r0 + sub if reverse else r0 - 1
            ref = b[edge:edge + 1, :]
            q_t = (q_r * jnp.exp2(b_r - ref)).astype(BF16)
            k_t = (k * jnp.exp2(jnp.minimum(ref - b, 0.0))).astype(BF16)
            a_off = lax.dot_general(q_t, k_t, (((1,), (1,)), ((), ())), preferred_element_type=F32)
            keep = (lane >= r0 + sub) if reverse else (lane < r0)
            a_blk = jnp.where(keep, a_off, 0.0)
        for jj in range(sub):
            j = r0 + jj
            w = q_r * k[j:j + 1, :] * jnp.exp2(jnp.minimum(b_r - b[j:j + 1, :], 0.0))
            s = jnp.sum(w, axis=1, keepdims=True)
            keep = (lrow < jj) if reverse else (lrow >= jj)
            a_blk = a_blk + jnp.where((lane == j) & keep, s, 0.0)
        blocks.append(a_blk)
    a = jnp.concatenate(blocks, axis=0)
    return o + jnp.dot(a.astype(BF16), v.astype(BF16), preferred_element_type=F32)


def _gla_fwd_kernel(q_ref, k_ref, v_ref, lg_ref, o_ref, st_ref, *, blocks_per_seq):
    @pl.when(pl.program_id(1) % blocks_per_seq == 0)
    def _():
        st_ref[...] = jnp.zeros_like(st_ref)

    def body(ci, carry):
        rows = pl.ds(pl.multiple_of(ci * GLA_CHUNK, GLA_CHUNK), GLA_CHUNK)
        o_ref[rows, :] = _gla_chunk(q_ref[rows, :], k_ref[rows, :], v_ref[rows, :], lg_ref[rows, :],
                                    st_ref, False)
        return carry

    lax.fori_loop(0, GLA_BLOCK // GLA_CHUNK, body, 0, unroll=2)


def _gla_bwd_kernel(q_ref, k_ref, v_ref, lg_ref, of_ref, r_ref, g_ref, y_ref, st_ref, *, blocks_per_seq):
    @pl.when(pl.program_id(1) % blocks_per_seq == 0)
    def _():
        st_ref[...] = jnp.zeros_like(st_ref)

    n_chunks = GLA_BLOCK // GLA_CHUNK

    def body(ci, carry):
        rows = pl.ds(pl.multiple_of((n_chunks - 1 - ci) * GLA_CHUNK, GLA_CHUNK), GLA_CHUNK)
        o = of_ref[rows, :] + _gla_chunk(q_ref[rows, :], k_ref[rows, :], v_ref[rows, :], lg_ref[rows, :],
                                         st_ref, True)
        o = o * lax.rsqrt(jnp.mean(o * o, axis=-1, keepdims=True) + RMS_EPS) * g_ref[...]
        r = r_ref[rows, :]
        y_ref[rows, :] = (o * (r * _sigmoid(r))).astype(y_ref.dtype)
        return carry

    lax.fori_loop(0, n_chunks, body, 0, unroll=2)


def _gla(pa, lg, norm_g, seq):
    m = pa.shape[0]
    nb = m // GLA_BLOCK
    bps = seq // GLA_BLOCK
    hk = GLA_HEADS
    blk = GLA_BLOCK

    def specs(row_of):
        return [pl.BlockSpec((blk, GLA_DK), lambda h, i: (row_of(i), h)),
                pl.BlockSpec((blk, GLA_DK), lambda h, i: (row_of(i), hk + h)),
                pl.BlockSpec((blk, GLA_DV), lambda h, i: (row_of(i), hk + h))]

    fwd_row = lambda i: i
    bwd_row = lambda i: nb - 1 - i
    scratch = [pltpu.VMEM((GLA_DV, GLA_DK), F32)]
    o_f = pl.pallas_call(
        functools.partial(_gla_fwd_kernel, blocks_per_seq=bps),
        out_shape=jax.ShapeDtypeStruct((m, GLA_HEADS * GLA_DV), F32),
        grid=(GLA_HEADS, nb),
        in_specs=specs(fwd_row) + [pl.BlockSpec((blk, GLA_DK), lambda h, i: (i, h))],
        out_specs=pl.BlockSpec((blk, GLA_DV), lambda h, i: (i, h)),
        scratch_shapes=scratch,
        compiler_params=_params(("parallel", "arbitrary")),
        name="gla_fwd",
    )(pa, pa, pa, lg)
    return pl.pallas_call(
        functools.partial(_gla_bwd_kernel, blocks_per_seq=bps),
        out_shape=jax.ShapeDtypeStruct((m, GLA_HEADS * GLA_DV), BF16),
        grid=(GLA_HEADS, nb),
        in_specs=specs(bwd_row) + [
            pl.BlockSpec((blk, GLA_DK), lambda h, i: (bwd_row(i), hk + h)),
            pl.BlockSpec((blk, GLA_DV), lambda h, i: (bwd_row(i), h)),
            pl.BlockSpec((blk, GLA_DV), lambda h, i: (bwd_row(i), 2 * hk + h)),
            pl.BlockSpec((1, GLA_DV), lambda h, i: (0, 0))],
        out_specs=pl.BlockSpec((blk, GLA_DV), lambda h, i: (bwd_row(i), h)),
        scratch_shapes=scratch,
        compiler_params=_params(("parallel", "arbitrary")),
        name="gla_bwd",
    )(pa, pa, pa, lg, o_f, pa, norm_g)


def _dil_kernel(q_ref, kp_ref, kc_ref, kn_ref, vp_ref, vc_ref, vn_ref, o_ref, l_ref, *, rows_per_seq):
    t, r, qs = DIL_TILE, DIL_RADIUS, DIL_QSUB
    start = pl.program_id(1) * t
    seq_lo = (start // rows_per_seq) * rows_per_seq
    row = lax.broadcasted_iota(jnp.int32, (qs, qs + 2 * r), 0)
    col = lax.broadcasted_iota(jnp.int32, (qs, qs + 2 * r), 1)
    rel = col - row
    in_band = (rel >= 0) & (rel <= 2 * r)
    valid = []
    for u in range(t // qs):
        pos = start + u * qs - r + col
        valid.append(in_band & (pos >= seq_lo) & (pos < seq_lo + rows_per_seq))
    head_lane = lax.broadcasted_iota(jnp.int32, (qs, LANE), 1)
    lse = [jnp.zeros((qs, LANE), F32) for _ in range(t // qs)]
    for h in range(DIL_HEADS):
        sl = slice(h * DIL_HD, (h + 1) * DIL_HD)
        kw = jnp.concatenate([kp_ref[:, sl], kc_ref[:, sl], kn_ref[:, sl]], axis=0)
        vw = jnp.concatenate([vp_ref[:, sl], vc_ref[:, sl], vn_ref[:, sl]], axis=0)
        for u in range(t // qs):
            rows = slice(u * qs, (u + 1) * qs)
            win = slice(u * qs, (u + 1) * qs + 2 * r)
            s = lax.dot_general(q_ref[rows, sl], kw[win], (((1,), (1,)), ((), ())), preferred_element_type=F32)
            s = jnp.where(valid[u], s, NEG_INF)
            m = jnp.max(s, axis=-1, keepdims=True)
            p = jnp.exp(s - m)
            den = jnp.sum(p, axis=-1, keepdims=True)
            o_ref[rows, sl] = jnp.dot(p.astype(BF16), vw[win], preferred_element_type=F32) / den
            lse[u] = jnp.where(head_lane == h, m + jnp.log(den), lse[u])
    for u in range(t // qs):
        l_ref[u * qs:(u + 1) * qs, :] = lse[u]


def _dilated_group(qkv, g, seq):
    d, rows, _ = qkv.shape
    t, r = DIL_TILE, DIL_RADIUS
    w = DIL_HEADS * DIL_HD
    sub = t // r
    last = rows // r - 1

    def prev_blk(i):
        return jnp.maximum(i * sub - 1, 0)

    def next_blk(i):
        return jnp.minimum((i + 1) * sub, last)

    def cur(col):
        return pl.BlockSpec((None, t, w), lambda res, i: (res, i, col))

    def halo(col, blk):
        return pl.BlockSpec((None, r, w), lambda res, i: (res, blk(i), col))

    return pl.pallas_call(
        functools.partial(_dil_kernel, rows_per_seq=seq // d),
        out_shape=(jax.ShapeDtypeStruct((d, rows, w), F32), jax.ShapeDtypeStruct((d, rows, LANE), F32)),
        grid=(d, rows // t),
        in_specs=[cur(0), halo(1, prev_blk), cur(1), halo(1, next_blk), halo(2, prev_blk), cur(2), halo(2, next_blk)],
        out_specs=(pl.BlockSpec((None, t, w), lambda res, i: (res, i, 0)),
                   pl.BlockSpec((None, t, LANE), lambda res, i: (res, i, 0))),
        compiler_params=_params(("parallel", "arbitrary")),
        name=f"dilated_{g}",
    )(qkv, qkv, qkv, qkv, qkv, qkv, qkv)


def _dil_combine_kernel(o0, o1, o2, l0, l1, l2, y_ref, so1, so2, sl1, sl2):
    def token_order(ref, buf, lanes=slice(None)):
        d, rows, _ = ref.shape
        if d == 1:
            return ref[0, :, lanes]
        for r in range(d):
            buf[pl.ds(r, rows, stride=d), :] = ref[r, :, lanes]
        return buf[...]

    a, b, c = token_order(l0, None), token_order(l1, sl1), token_order(l2, sl2)
    mx = jnp.maximum(jnp.maximum(a, b), c)
    ea, eb, ec = jnp.exp(a - mx), jnp.exp(b - mx), jnp.exp(c - mx)
    inv = 1.0 / (ea + eb + ec)
    wa, wb, wc = ea * inv, eb * inv, ec * inv
    for h in range(DIL_HEADS):
        lanes = slice(h * DIL_HD, (h + 1) * DIL_HD)
        y = (wa[:, h:h + 1] * token_order(o0, None, lanes) + wb[:, h:h + 1] * token_order(o1, so1.at[h], lanes)
             + wc[:, h:h + 1] * token_order(o2, so2.at[h], lanes))
        y_ref[:, lanes] = y.astype(y_ref.dtype)


def _dil_combine(outs, lses, tm):
    w = outs[0].shape[2]
    m = outs[0].shape[0] * outs[0].shape[1]
    specs = [pl.BlockSpec((a.shape[0], tm // a.shape[0], a.shape[2]), lambda i: (0, i, 0)) for a in (*outs, *lses)]
    return pl.pallas_call(
        _dil_combine_kernel,
        out_shape=jax.ShapeDtypeStruct((m, w), BF16),
        grid=(m // tm,),
        in_specs=specs,
        out_specs=pl.BlockSpec((tm, w), lambda i: (i, 0)),
        scratch_shapes=[pltpu.VMEM((DIL_HEADS, tm, DIL_HD), F32)] * 2 + [pltpu.VMEM((tm, LANE), F32)] * 2,
        compiler_params=_params(("parallel",)),
        name="dilated_combine",
    )(*outs, *lses)


def _rope64(x, cos, sin_a, sin_b):
    return x * cos + pltpu.roll(x, 96, 1) * sin_a + pltpu.roll(x, 32, 1) * sin_b


def _mla_post_kernel(c_ref, gq_ref, gkv_ref, wuq_ref, wkn_ref, wv_ref, cos_ref, sa_ref, sb_ref,
                     q_ref, k_ref, v_ref):
    c = c_ref[...]
    cq = c[:, :MLA_QRANK]
    cq = cq * lax.rsqrt(jnp.mean(cq * cq, axis=-1, keepdims=True) + RMS_EPS) * gq_ref[...]
    ckv = c[:, MLA_QRANK:MLA_QRANK + MLA_KVRANK]
    ckv = (ckv * lax.rsqrt(jnp.mean(ckv * ckv, axis=-1, keepdims=True) + RMS_EPS) * gkv_ref[...]).astype(BF16)
    cos, sa, sb = cos_ref[...], sa_ref[...], sb_ref[...]
    scale = (MLA_NOPE + MLA_ROPE) ** -0.5
    q = jnp.dot(cq.astype(BF16), wuq_ref[...], preferred_element_type=F32) * scale
    kn = jnp.dot(ckv, wkn_ref[...], preferred_element_type=F32)
    kr = _rope64(c[:, MLA_QRANK + MLA_KVRANK:MLA_QRANK + MLA_KVRANK + LANE], cos, sa, sb).astype(BF16)
    for h in range(MLA_HEADS):
        base = h * MLA_HPAD
        q_ref[:, base:base + LANE] = q[:, base:base + LANE].astype(BF16)
        q_ref[:, base + LANE:base + 2 * LANE] = _rope64(q[:, base + LANE:base + 2 * LANE], cos, sa, sb).astype(BF16)
        k_ref[:, base:base + LANE] = kn[:, h * LANE:(h + 1) * LANE].astype(BF16)
        k_ref[:, base + LANE:base + 2 * LANE] = kr
    vv = jnp.dot(ckv, wv_ref[...], preferred_element_type=F32).astype(BF16)
    ones = jnp.ones((vv.shape[0], LANE), BF16)
    for h in range(MLA_HEADS):
        base = h * MLA_HPAD
        v_ref[:, base:base + LANE] = vv[:, h * LANE:(h + 1) * LANE]
        v_ref[:, base + LANE:base + 2 * LANE] = ones


def _mla_post(c, gq, gkv, wuq, wkn, wv, cos, sa, sb, seq, tm):
    m = c.shape[0]
    tiles_per_seq = seq // tm
    const = lambda i: (0, 0)
    tab = pl.BlockSpec((tm, LANE), lambda i: (i % tiles_per_seq, 0))
    hw = MLA_HEADS * MLA_HPAD
    return pl.pallas_call(
        _mla_post_kernel,
        out_shape=(jax.ShapeDtypeStruct((m, hw), BF16),) * 3,
        grid=(m // tm,),
        in_specs=[pl.BlockSpec((tm, C_COLS), lambda i: (i, 0)),
                  pl.BlockSpec((1, MLA_QRANK), const), pl.BlockSpec((1, MLA_KVRANK), const),
                  pl.BlockSpec(wuq.shape, const), pl.BlockSpec(wkn.shape, const), pl.BlockSpec(wv.shape, const),
                  tab, tab, tab],
        out_specs=(pl.BlockSpec((tm, hw), lambda i: (i, 0)),) * 3,
        compiler_params=_params(("parallel",)),
        name="mla_post",
    )(c, gq, gkv, wuq, wkn, wv, cos, sa, sb)


def _mla_attn_kernel(q_ref, k_ref, v_ref, o_ref, m_ref, acc_ref, s_ref, *, tk, sub):
    m_ref[...] = jnp.full_like(m_ref, -jnp.inf)
    acc_ref[...] = jnp.zeros_like(acc_ref)
    n_sub = q_ref.shape[0] // sub
    n_chunks = k_ref.shape[0] // tk
    nt = (((1,), (1,)), ((), ()))

    def chunk_rows(ci):
        return pl.ds(pl.multiple_of(ci * tk, tk), tk)

    def step(slot, ci):
        k_next = k_ref[chunk_rows(jnp.minimum(ci + 1, n_chunks - 1)), :]
        v_cur = v_ref[chunk_rows(ci), :]
        for r in range(n_sub):
            rs = slice(r * sub, (r + 1) * sub)
            s_ref[1 - slot, rs, :] = lax.dot_general(q_ref[rs, :], k_next, nt, preferred_element_type=F32)
            s = s_ref[slot, rs, :]
            m_prev = m_ref[rs, :]
            m_new = jnp.maximum(m_prev, jnp.max(s, axis=-1, keepdims=True))
            alpha = jnp.exp(m_prev - m_new)
            p = jnp.exp(s - jnp.concatenate([m_new] * (tk // LANE), axis=1))
            acc_ref[rs, :] = (jnp.concatenate([alpha, alpha], axis=1) * acc_ref[rs, :]
                              + jnp.dot(p.astype(BF16), v_cur, preferred_element_type=F32))
            m_ref[rs, :] = m_new

    def body(i, carry):
        for c in range(MLA_CHUNKS_PER_TRIP):
            step(c % 2, MLA_CHUNKS_PER_TRIP * i + c)
        return carry

    k_first = k_ref[chunk_rows(0), :]
    for r in range(n_sub):
        rs = slice(r * sub, (r + 1) * sub)
        s_ref[0, rs, :] = lax.dot_general(q_ref[rs, :], k_first, nt, preferred_element_type=F32)
    lax.fori_loop(0, n_chunks // MLA_CHUNKS_PER_TRIP, body, 0)
    acc = acc_ref[...]
    o_ref[...] = (acc[:, :MLA_V] / acc[:, MLA_V:]).astype(o_ref.dtype)


def _mla_attn(q, k, v, seq, tq, tk, sub=256):
    m = q.shape[0]
    nseq = m // seq
    qt = seq // tq
    assert seq % (MLA_CHUNKS_PER_TRIP * tk) == 0 and tq % min(sub, tq) == 0
    return pl.pallas_call(
        functools.partial(_mla_attn_kernel, tk=tk, sub=min(sub, tq)),
        out_shape=jax.ShapeDtypeStruct((m, MLA_HEADS * MLA_V), BF16),
        grid=(nseq, MLA_HEADS, qt),
        in_specs=[pl.BlockSpec((tq, MLA_HPAD), lambda b, h, i: (b * qt + i, h)),
                  pl.BlockSpec((seq, MLA_HPAD), lambda b, h, i: (b, h)),
                  pl.BlockSpec((seq, MLA_HPAD), lambda b, h, i: (b, h))],
        out_specs=pl.BlockSpec((tq, MLA_V), lambda b, h, i: (b * qt + i, h)),
        scratch_shapes=[pltpu.VMEM((tq, LANE), F32), pltpu.VMEM((tq, MLA_HPAD), F32),
                        pltpu.VMEM((2, tq, tk), F32)],
        compiler_params=_params(("parallel", "parallel", "arbitrary")),
        name="mla_attn",
    )(q, k, v)


def _merge_kernel(x_ref, wg0, wg1, wg2, ya, yb, yc, wb0, wb1, wb2, o_ref):
    x = x_ref[...]
    acc = None
    for wg, y, wb in ((wg0, ya, wb0), (wg1, yb, wb1), (wg2, yc, wb2)):
        gate = _sigmoid(jnp.dot(x, wg[...], preferred_element_type=F32))
        term = gate * jnp.dot(y[...], wb[...], preferred_element_type=F32)
        acc = term if acc is None else acc + term
    o_ref[...] = acc.astype(o_ref.dtype)


def _merge(x, w_gate, ys, w_branch, tm, tn):
    m, k = x.shape
    n = w_branch.shape[2]
    nj = n // tn
    kb = w_branch.shape[1]
    gate_specs = [pl.BlockSpec((k, tn), functools.partial(lambda i, j, b: (0, b * nj + j), b=b)) for b in range(3)]
    y_specs = [pl.BlockSpec((tm, kb), lambda i, j: (i, 0))] * 3
    br_specs = [pl.BlockSpec((None, kb, tn), functools.partial(lambda i, j, b: (b, 0, j), b=b)) for b in range(3)]
    return pl.pallas_call(
        _merge_kernel,
        out_shape=jax.ShapeDtypeStruct((m, n), BF16),
        grid=(m // tm, nj),
        in_specs=[pl.BlockSpec((tm, k), lambda i, j: (i, 0))] + gate_specs + y_specs + br_specs,
        out_specs=pl.BlockSpec((tm, tn), lambda i, j: (i, j)),
        compiler_params=_params(("parallel", "arbitrary")),
        name="merge",
    )(x, w_gate, w_gate, w_gate, *ys, w_branch, w_branch, w_branch)


def _prep_stacks(w_out, ffn_g, ffn_u, ffn_d):
    fpad = D_FF_PAD - D_FF
    return dict(
        w_out=w_out.astype(BF16),
        ffn_g=jnp.pad(ffn_g, ((0, 0), (0, 0), (0, 0), (0, fpad))).astype(BF16),
        ffn_u=jnp.pad(ffn_u, ((0, 0), (0, 0), (0, 0), (0, fpad))).astype(BF16),
        ffn_d=jnp.pad(ffn_d, ((0, 0), (0, 0), (0, fpad), (0, 0))).astype(BF16),
    )


def _prep_layer(w_in, gate_w, gate_b, norm_g, q_norm_g, uq, kv_norm_g, ukv, w_branch, ln_g, ln_b):
    d = D_MODEL
    n_a = 2 * GLA_HEADS * GLA_DK + 2 * GLA_HEADS * GLA_DV
    n_rank = 2 * GLA_RANK
    n_b = len(DIL_DILATIONS) * 3 * BRANCH_W
    n_c = MLA_QRANK + MLA_KVRANK + MLA_ROPE
    o_b = n_a + n_rank
    o_c = o_b + n_b
    o_g = o_c + n_c
    w_c = jnp.concatenate([w_in[:, o_c:o_g], jnp.zeros((d, LANE - MLA_ROPE), F32),
                           w_in[:, n_a:o_b], jnp.zeros((d, LANE - n_rank), F32)], axis=1)
    gw = jnp.zeros((LANE, 2 * GLA_HEADS * GLA_DK), F32)
    gw = gw.at[:GLA_RANK, :GLA_HEADS * GLA_DK].set(gate_w[0])
    gw = gw.at[GLA_RANK:2 * GLA_RANK, GLA_HEADS * GLA_DK:].set(gate_w[1])
    uq_pad = jnp.pad(uq.reshape(MLA_QRANK, MLA_HEADS, MLA_NOPE + MLA_ROPE),
                     ((0, 0), (0, 0), (0, MLA_HPAD - MLA_NOPE - MLA_ROPE))).reshape(MLA_QRANK, MLA_HEADS * MLA_HPAD)
    ukv3 = ukv.reshape(MLA_KVRANK, MLA_HEADS, MLA_NOPE + MLA_V)
    return dict(
        w_a=w_in[:, :n_a].astype(BF16),
        w_dil=[w_in[:, o_b + g * 3 * BRANCH_W:o_b + (g + 1) * 3 * BRANCH_W].astype(BF16)
               for g in range(len(DIL_DILATIONS))],
        w_c=w_c.astype(BF16),
        w_gate=w_in[:, o_g:].astype(BF16),
        gla_gw=gw.astype(BF16),
        gla_gb=jnp.concatenate([gate_b[0], gate_b[1]])[None, :],
        gla_norm=norm_g[None, :],
        q_norm=q_norm_g[None, :],
        kv_norm=kv_norm_g[None, :],
        w_uq=uq_pad.astype(BF16),
        w_kn=ukv3[:, :, :MLA_NOPE].reshape(MLA_KVRANK, -1).astype(BF16),
        w_vv=ukv3[:, :, MLA_NOPE:].reshape(MLA_KVRANK, -1).astype(BF16),
        w_branch=w_branch.astype(BF16),
        ln_g=ln_g[:, None, :],
        ln_b=ln_b[:, None, :],
    )


def _rope_tables(seq):
    pos = jnp.arange(seq, dtype=F32)[:, None]
    half = DIL_HD // 2
    ang = pos * jnp.power(ROPE_THETA, -2.0 * jnp.arange(half, dtype=F32) / DIL_HD)[None, :]
    cos128 = jnp.concatenate([jnp.cos(ang), jnp.cos(ang)], axis=1)
    sin128 = jnp.concatenate([-jnp.sin(ang), jnp.sin(ang)], axis=1)
    half = MLA_ROPE // 2
    ang = pos * jnp.power(ROPE_THETA, -2.0 * jnp.arange(half, dtype=F32) / MLA_ROPE)[None, :]
    zero = jnp.zeros((seq, half), F32)
    pad = jnp.zeros((seq, LANE - MLA_ROPE), F32)
    cos64 = jnp.concatenate([jnp.cos(ang), jnp.cos(ang), pad], axis=1)
    sin_a = jnp.concatenate([-jnp.sin(ang), zero, pad], axis=1)
    sin_b = jnp.concatenate([zero, jnp.sin(ang), pad], axis=1)
    return cos128, sin128, cos64, sin_a, sin_b


def _layer(x, xb, l, p, stk, tabs, seq):
    cos128, sin128, cos64, sin_a, sin_b = tabs
    h = _ffn_up(xb, stk["ffn_g"], stk["ffn_u"], (l, 0), 1024, 512)
    x, xb = _mm_res_ln(h, stk["ffn_d"], (l, 0), x, p["ln_g"][0], p["ln_b"][0], 0.5, 512, 512, "ffn_down_ln")

    pa = _proj(xb, p["w_a"], F32, 1024, 512, "proj_gla")
    c = _proj(xb, p["w_c"], F32, 512, C_COLS, "proj_c")

    lg = _gla_gates(c, p["gla_gw"], p["gla_gb"], 512)
    y_a = _gla(pa, lg, p["gla_norm"], seq)

    outs, lses = [], []
    for g, d in enumerate(DIL_DILATIONS):
        qkv = _proj_dil(xb, p["w_dil"][g], cos128, sin128, seq, d, 1024, 1024, f"proj_dil_{g}")
        o, lse = _dilated_group(qkv, g, seq)
        outs.append(o)
        lses.append(lse)
    y_b = _dil_combine(outs, lses, 256)

    q_pad, k_pad, v_m = _mla_post(c, p["q_norm"], p["kv_norm"], p["w_uq"], p["w_kn"], p["w_vv"],
                                  cos64, sin_a, sin_b, seq, 256)
    y_c = _mla_attn(q_pad, k_pad, v_m, seq, 1024, 1024)

    merged = _merge(xb, p["w_gate"], (y_a, y_b, y_c), p["w_branch"], 512, 256)
    x, xb = _mm_res_ln(merged, stk["w_out"], (l,), x, p["ln_g"][1], p["ln_b"][1], 1.0, 512, 512, "out_proj_ln")

    h = _ffn_up(xb, stk["ffn_g"], stk["ffn_u"], (l, 1), 1024, 512)
    return _mm_res_ln(h, stk["ffn_d"], (l, 1), x, p["ln_g"][2], p["ln_b"][2], 0.5, 512, 512, "ffn_down_ln")


def _trunk(x3, layers, stk):
    bsz, seq, d = x3.shape
    x = x3.reshape(bsz * seq, d)
    xb = x.astype(BF16)
    tabs = _rope_tables(seq)
    for l, p in enumerate(layers):
        x, xb = _layer(x, xb, l, p, stk, tabs, seq)
    return x.reshape(bsz, seq, d)


def kernel(x_prompt, x_sample, w_in, gla_gate_w, gla_gate_b, gla_norm_g, mla_q_norm_g, mla_uq, mla_kv_norm_g,
           mla_ukv, w_branch, w_out, ffn_w_gate, ffn_w_up, ffn_w_down, ln_g, ln_b):
    layers = [_prep_layer(w_in[l], gla_gate_w[l], gla_gate_b[l], gla_norm_g[l], mla_q_norm_g[l], mla_uq[l],
                          mla_kv_norm_g[l], mla_ukv[l], w_branch[l], ln_g[l], ln_b[l]) for l in range(N_LAYERS)]
    stk = _prep_stacks(w_out, ffn_w_gate, ffn_w_up, ffn_w_down)
    return (_trunk(x_prompt, layers, stk), _trunk(x_sample, layers, stk))
```

```python
import functools
import math

import jax
import jax.numpy as jnp
from jax import lax
from jax.experimental import pallas as pl
from jax.experimental.pallas import tpu as pltpu

F32 = jnp.float32
BF16 = jnp.bfloat16

D_MODEL = 4096
N_LAYERS = 2
D_FF = 11008
D_FF_PAD = 11264
FFN_SUB_N = 256
ROPE_THETA = 10000.0
LN_EPS = 1e-5
RMS_EPS = 1e-6
NEG_INF = -1e30
DEEPNORM_ALPHA = (2 * N_LAYERS) ** 0.25
BRANCH_W = 2048

GLA_HEADS = 4
GLA_DK = 256
GLA_DV = 512
GLA_RANK = 16
GLA_NORMALIZER = 16.0
GLA_CHUNK = 64
GLA_SUB = 16
GLA_BLOCK = 256

DIL_DILATIONS = (1, 4, 16)
DIL_RADIUS = 64
DIL_HEADS = 16
DIL_HD = 128
DIL_TILE = 256
DIL_QSUB = 128
PERM_ROWS = 256

MLA_HEADS = 16
MLA_NOPE = 128
MLA_ROPE = 64
MLA_V = 128
MLA_QRANK = 896
MLA_KVRANK = 256
MLA_HPAD = 256
MLA_CHUNKS_PER_TRIP = 4
C_COLS = 1408
C_GATE_BLOCK = 10

LANE = 128
VMEM_LIMIT = 56 * 1024 * 1024
VMEM_LIMIT_WIDE = 60000 * 1024


def _params(sem, vmem=VMEM_LIMIT):
    return pltpu.CompilerParams(dimension_semantics=sem, vmem_limit_bytes=vmem)


def _sigmoid(x):
    return 1.0 / (1.0 + jnp.exp(-x))


def _proj_kernel(x_ref, w_ref, o_ref):
    o_ref[...] = jnp.dot(x_ref[...], w_ref[...], preferred_element_type=F32).astype(o_ref.dtype)


def _proj(x, w, out_dtype, tm, tn, name):
    m, k = x.shape
    n = w.shape[1]
    return pl.pallas_call(
        _proj_kernel,
        out_shape=jax.ShapeDtypeStruct((m, n), out_dtype),
        grid=(m // tm, n // tn),
        in_specs=[pl.BlockSpec((tm, k), lambda i, j: (i, 0)),
                  pl.BlockSpec((k, tn), lambda i, j: (0, j))],
        out_specs=pl.BlockSpec((tm, tn), lambda i, j: (i, j)),
        compiler_params=_params(("parallel", "arbitrary")),
        name=name,
    )(x, w)


def _proj_dil_kernel(x_ref, w_ref, cos_ref, sin_ref, o_ref, *, d, n_q_tiles, n_rope_tiles, q_scale, sub_n):
    j = pl.program_id(1)
    x = x_ref[...]
    is_rope = j < n_rope_tiles
    scale = jnp.where(j < n_q_tiles, q_scale, 1.0).astype(F32)
    cos = jnp.where(is_rope, cos_ref[...] * scale, 1.0)
    sin = jnp.where(is_rope, sin_ref[...] * scale, 0.0)
    if d > 1:
        per = PERM_ROWS // d
        out_row = lax.broadcasted_iota(jnp.int32, (PERM_ROWS, PERM_ROWS), 0)
        in_row = lax.broadcasted_iota(jnp.int32, (PERM_ROWS, PERM_ROWS), 1)
        src = (out_row & (per - 1)) * d + lax.shift_right_logical(out_row, per.bit_length() - 1)
        perm = jnp.where(in_row == src, 1.0, 0.0).astype(BF16)
    for t in range(w_ref.shape[1] // sub_n):
        cols = slice(t * sub_n, (t + 1) * sub_n)
        acc = jnp.dot(x, w_ref[:, cols], preferred_element_type=F32)
        heads = []
        for h in range(sub_n // LANE):
            y = acc[:, h * LANE:(h + 1) * LANE]
            heads.append((y * cos + pltpu.roll(y, LANE // 2, 1) * sin).astype(BF16))
        res = jnp.concatenate(heads, axis=1)
        if d == 1:
            o_ref[0, :, cols] = res
            continue
        for b in range(res.shape[0] // PERM_ROWS):
            moved = jnp.dot(perm, res[b * PERM_ROWS:(b + 1) * PERM_ROWS, :],
                            preferred_element_type=F32).astype(o_ref.dtype)
            for r in range(d):
                o_ref[r, b * per:(b + 1) * per, cols] = moved[r * per:(r + 1) * per, :]


def _proj_dil(x, w, cos, sin, seq, d, tm, tn, name, sub_n=512):
    m, k = x.shape
    n = w.shape[1]
    tiles_per_seq = seq // tm
    w_head = DIL_HEADS * DIL_HD
    return pl.pallas_call(
        functools.partial(_proj_dil_kernel, d=d, n_q_tiles=w_head // tn, n_rope_tiles=2 * w_head // tn,
                          q_scale=DIL_HD ** -0.5, sub_n=min(sub_n, tn)),
        out_shape=jax.ShapeDtypeStruct((d, m // d, n), BF16),
        grid=(m // tm, n // tn),
        in_specs=[pl.BlockSpec((tm, k), lambda i, j: (i, 0)),
                  pl.BlockSpec((k, tn), lambda i, j: (0, j)),
                  pl.BlockSpec((tm, LANE), lambda i, j: (i % tiles_per_seq, 0)),
                  pl.BlockSpec((tm, LANE), lambda i, j: (i % tiles_per_seq, 0))],
        out_specs=pl.BlockSpec((d, tm // d, tn), lambda i, j: (0, i, j)),
        compiler_params=_params(("parallel", "arbitrary")),
        name=name,
    )(x, w, cos, sin)


def _ffn_up_kernel(x_ref, wg_ref, wu_ref, o_ref):
    x = x_ref[...]
    for t in range(o_ref.shape[1] // FFN_SUB_N):
        cols = slice(t * FFN_SUB_N, (t + 1) * FFN_SUB_N)
        g = jnp.dot(x, wg_ref[:, cols], preferred_element_type=F32)
        u = jnp.dot(x, wu_ref[:, cols], preferred_element_type=F32)
        o_ref[:, cols] = (g * _sigmoid(g) * u).astype(o_ref.dtype)


def _stacked_spec(lead, block, index_map):
    return pl.BlockSpec((None,) * len(lead) + tuple(block), lambda *g: tuple(lead) + tuple(index_map(*g)))


def _ffn_up(x, wg, wu, lead, tm, tn):
    m, k = x.shape
    n = wg.shape[-1]
    w_spec = _stacked_spec(lead, (k, tn), lambda i, j: (0, j))
    return pl.pallas_call(
        _ffn_up_kernel,
        out_shape=jax.ShapeDtypeStruct((m, n), BF16),
        grid=(m // tm, n // tn),
        in_specs=[pl.BlockSpec((tm, k), lambda i, j: (i, 0)), w_spec, w_spec],
        out_specs=pl.BlockSpec((tm, tn), lambda i, j: (i, j)),
        compiler_params=_params(("parallel", "arbitrary")),
        name="ffn_up",
    )(x, wg, wu)


def _mm_res_ln_kernel(a_ref, w_ref, x_ref, g_ref, b_ref, of_ref, ob_ref, *, scale):
    kk = pl.program_id(1)

    @pl.when(kk == 0)
    def _():
        of_ref[...] = jnp.dot(a_ref[...], w_ref[...], preferred_element_type=F32)

    @pl.when(kk > 0)
    def _():
        of_ref[...] += jnp.dot(a_ref[...], w_ref[...], preferred_element_type=F32)

    @pl.when(kk == pl.num_programs(1) - 1)
    def _():
        z = DEEPNORM_ALPHA * x_ref[...] + scale * of_ref[...]
        mu = jnp.mean(z, axis=-1, keepdims=True)
        zc = z - mu
        var = jnp.mean(zc * zc, axis=-1, keepdims=True)
        y = zc * lax.rsqrt(var + LN_EPS) * g_ref[...] + b_ref[...]
        of_ref[...] = y
        ob_ref[...] = y.astype(BF16)


def _mm_res_ln(a, w, lead, x, g, b, scale, tm, tk, name):
    m, k = a.shape
    n = w.shape[-1]
    return pl.pallas_call(
        functools.partial(_mm_res_ln_kernel, scale=scale),
        out_shape=(jax.ShapeDtypeStruct((m, n), F32), jax.ShapeDtypeStruct((m, n), BF16)),
        grid=(m // tm, k // tk),
        in_specs=[pl.BlockSpec((tm, tk), lambda i, kk: (i, kk)),
                  _stacked_spec(lead, (tk, n), lambda i, kk: (kk, 0)),
                  pl.BlockSpec((tm, n), lambda i, kk: (i, 0)),
                  pl.BlockSpec((1, n), lambda i, kk: (0, 0)),
                  pl.BlockSpec((1, n), lambda i, kk: (0, 0))],
        out_specs=(pl.BlockSpec((tm, n), lambda i, kk: (i, 0)),
                   pl.BlockSpec((tm, n), lambda i, kk: (i, 0))),
        compiler_params=_params(("parallel", "arbitrary"), VMEM_LIMIT_WIDE),
        name=name,
    )(a, w, x, g, b)


def _gla_gate_kernel(c_ref, gw_ref, gb_ref, o_ref):
    z = jnp.dot(c_ref[...].astype(BF16), gw_ref[...], preferred_element_type=F32) + gb_ref[...]
    log_sig = jnp.minimum(z, 0.0) - jnp.log(1.0 + jnp.exp(-jnp.abs(z)))
    o_ref[...] = log_sig * (1.0 / GLA_NORMALIZER)


def _gla_gates(c, gw, gb, tm):
    m = c.shape[0]
    n = gw.shape[1]
    return pl.pallas_call(
        _gla_gate_kernel,
        out_shape=jax.ShapeDtypeStruct((m, n), F32),
        grid=(m // tm,),
        in_specs=[pl.BlockSpec((tm, LANE), lambda i: (i, C_GATE_BLOCK)),
                  pl.BlockSpec((LANE, n), lambda i: (0, 0)),
                  pl.BlockSpec((1, n), lambda i: (0, 0))],
        out_specs=pl.BlockSpec((tm, n), lambda i: (i, 0)),
        compiler_params=_params(("parallel",)),
        name="gla_gates",
    )(c, gw, gb)


def _gla_chunk(q, k, v, lg, st_ref, reverse):
    c, sub = GLA_CHUNK, GLA_SUB
    nsub = c // sub
    row = lax.broadcasted_iota(jnp.int32, (c, c), 0)
    col = lax.broadcasted_iota(jnp.int32, (c, c), 1)
    tri = jnp.where((col >= row) if reverse else (col <= row), 1.0, 0.0).astype(BF16)
    hi = lg.astype(BF16)
    r1 = lg - hi.astype(F32)
    mid = r1.astype(BF16)
    lo = (r1 - mid.astype(F32)).astype(BF16)
    b = (jnp.dot(tri, hi, preferred_element_type=F32) + jnp.dot(tri, mid, preferred_element_type=F32)
         + jnp.dot(tri, lo, preferred_element_type=F32))
    b = b * math.log2(math.e)
    total = b[0:1, :] if reverse else b[c - 1:c, :]

    q = q * (GLA_DK ** -0.5)
    st = st_ref[...]
    o = lax.dot_general((q * jnp.exp2(b)).astype(BF16), st.astype(BF16), (((1,), (1,)), ((), ())),
                        preferred_element_type=F32)
    k_state = (k * jnp.exp2(total - b)).astype(BF16)
    st_ref[...] = st * jnp.exp2(total) + lax.dot_general(
        v.astype(BF16), k_state, (((0,), (0,)), ((), ())), preferred_element_type=F32)

    lane = lax.broadcasted_iota(jnp.int32, (sub, c), 1)
    lrow = lax.broadcasted_iota(jnp.int32, (sub, c), 0)
    blocks = []
    for i_sub in range(nsub):
        r0 = i_sub * sub
        q_r = q[r0:r0 + sub, :]
        b_r = b[r0:r0 + sub, :]
        a_blk = jnp.zeros((sub, c), F32)
        has_off = (i_sub < nsub - 1) if reverse else (i_sub > 0)
        if has_off:
            edge = r0 + sub if reverse else r0 - 1
            ref = b[edge:edge + 1, :]
            q_t = (q_r * jnp.exp2(b_r - ref)).astype(BF16)
            k_t = (k * jnp.exp2(jnp.minimum(ref - b, 0.0))).astype(BF16)
            a_off = lax.dot_general(q_t, k_t, (((1,), (1,)), ((), ())), preferred_element_type=F32)
            keep = (lane >= r0 + sub) if reverse else (lane < r0)
            a_blk = jnp.where(keep, a_off, 0.0)
        for jj in range(sub):
            j = r0 + jj
            w = q_r * k[j:j + 1, :] * jnp.exp2(jnp.minimum(b_r - b[j:j + 1, :], 0.0))
            s = jnp.sum(w, axis=1, keepdims=True)
            keep = (lrow < jj) if reverse else (lrow >= jj)
            a_blk = a_blk + jnp.where((lane == j) & keep, s, 0.0)
        blocks.append(a_blk)
    a = jnp.concatenate(blocks, axis=0)
    return o + jnp.dot(a.astype(BF16), v.astype(BF16), preferred_element_type=F32)


def _gla_fwd_kernel(q_ref, k_ref, v_ref, lg_ref, o_ref, st_ref, *, blocks_per_seq):
    @pl.when(pl.program_id(1) % blocks_per_seq == 0)
    def _():
        st_ref[...] = jnp.zeros_like(st_ref)

    def body(ci, carry):
        rows = pl.ds(pl.multiple_of(ci * GLA_CHUNK, GLA_CHUNK), GLA_CHUNK)
        o_ref[rows, :] = _gla_chunk(q_ref[rows, :], k_ref[rows, :], v_ref[rows, :], lg_ref[rows, :],
                                    st_ref, False)
        return carry

    lax.fori_loop(0, GLA_BLOCK // GLA_CHUNK, body, 0, unroll=True)


def _gla_bwd_kernel(q_ref, k_ref, v_ref, lg_ref, of_ref, r_ref, g_ref, y_ref, st_ref, *, blocks_per_seq):
    @pl.when(pl.program_id(1) % blocks_per_seq == 0)
    def _():
        st_ref[...] = jnp.zeros_like(st_ref)

    n_chunks = GLA_BLOCK // GLA_CHUNK

    def body(ci, carry):
        rows = pl.ds(pl.multiple_of((n_chunks - 1 - ci) * GLA_CHUNK, GLA_CHUNK), GLA_CHUNK)
        o = of_ref[rows, :] + _gla_chunk(q_ref[rows, :], k_ref[rows, :], v_ref[rows, :], lg_ref[rows, :],
                                         st_ref, True)
        o = o * lax.rsqrt(jnp.mean(o * o, axis=-1, keepdims=True) + RMS_EPS) * g_ref[...]
        r = r_ref[rows, :]
        y_ref[rows, :] = (o * (r * _sigmoid(r))).astype(y_ref.dtype)
        return carry

    lax.fori_loop(0, n_chunks, body, 0, unroll=True)


def _gla(pa, lg, norm_g, seq):
    m = pa.shape[0]
    nb = m // GLA_BLOCK
    bps = seq // GLA_BLOCK
    hk = GLA_HEADS
    blk = GLA_BLOCK

    def specs(row_of):
        return [pl.BlockSpec((blk, GLA_DK), lambda h, i: (row_of(i), h)),
                pl.BlockSpec((blk, GLA_DK), lambda h, i: (row_of(i), hk + h)),
                pl.BlockSpec((blk, GLA_DV), lambda h, i: (row_of(i), hk + h))]

    fwd_row = lambda i: i
    bwd_row = lambda i: nb - 1 - i
    scratch = [pltpu.VMEM((GLA_DV, GLA_DK), F32)]
    o_f = pl.pallas_call(
        functools.partial(_gla_fwd_kernel, blocks_per_seq=bps),
        out_shape=jax.ShapeDtypeStruct((m, GLA_HEADS * GLA_DV), F32),
        grid=(GLA_HEADS, nb),
        in_specs=specs(fwd_row) + [pl.BlockSpec((blk, GLA_DK), lambda h, i: (i, h))],
        out_specs=pl.BlockSpec((blk, GLA_DV), lambda h, i: (i, h)),
        scratch_shapes=scratch,
        compiler_params=_params(("parallel", "arbitrary")),
        name="gla_fwd",
    )(pa, pa, pa, lg)
    return pl.pallas_call(
        functools.partial(_gla_bwd_kernel, blocks_per_seq=bps),
        out_shape=jax.ShapeDtypeStruct((m, GLA_HEADS * GLA_DV), BF16),
        grid=(GLA_HEADS, nb),
        in_specs=specs(bwd_row) + [
            pl.BlockSpec((blk, GLA_DK), lambda h, i: (bwd_row(i), hk + h)),
            pl.BlockSpec((blk, GLA_DV), lambda h, i: (bwd_row(i), h)),
            pl.BlockSpec((blk, GLA_DV), lambda h, i: (bwd_row(i), 2 * hk + h)),
            pl.BlockSpec((1, GLA_DV), lambda h, i: (0, 0))],
        out_specs=pl.BlockSpec((blk, GLA_DV), lambda h, i: (bwd_row(i), h)),
        scratch_shapes=scratch,
        compiler_params=_params(("parallel", "arbitrary")),
        name="gla_bwd",
    )(pa, pa, pa, lg, o_f, pa, norm_g)


def _dil_kernel(q_ref, kp_ref, kc_ref, kn_ref, vp_ref, vc_ref, vn_ref, o_ref, l_ref, *, rows_per_seq):
    t, r, qs = DIL_TILE, DIL_RADIUS, DIL_QSUB
    start = pl.program_id(1) * t
    seq_lo = (start // rows_per_seq) * rows_per_seq
    row = lax.broadcasted_iota(jnp.int32, (qs, qs + 2 * r), 0)
    col = lax.broadcasted_iota(jnp.int32, (qs, qs + 2 * r), 1)
    rel = col - row
    in_band = (rel >= 0) & (rel <= 2 * r)
    valid = []
    for u in range(t // qs):
        pos = start + u * qs - r + col
        valid.append(in_band & (pos >= seq_lo) & (pos < seq_lo + rows_per_seq))
    head_lane = lax.broadcasted_iota(jnp.int32, (qs, LANE), 1)
    lse = [jnp.zeros((qs, LANE), F32) for _ in range(t // qs)]
    for h in range(DIL_HEADS):
        sl = slice(h * DIL_HD, (h + 1) * DIL_HD)
        kw = jnp.concatenate([kp_ref[:, sl], kc_ref[:, sl], kn_ref[:, sl]], axis=0)
        vw = jnp.concatenate([vp_ref[:, sl], vc_ref[:, sl], vn_ref[:, sl]], axis=0)
        for u in range(t // qs):
            rows = slice(u * qs, (u + 1) * qs)
            win = slice(u * qs, (u + 1) * qs + 2 * r)
            s = lax.dot_general(q_ref[rows, sl], kw[win], (((1,), (1,)), ((), ())), preferred_element_type=F32)
            s = jnp.where(valid[u], s, NEG_INF)
            m = jnp.max(s, axis=-1, keepdims=True)
            p = jnp.exp(s - m)
            den = jnp.sum(p, axis=-1, keepdims=True)
            o_ref[rows, sl] = jnp.dot(p.astype(BF16), vw[win], preferred_element_type=F32) / den
            lse[u] = jnp.where(head_lane == h, m + jnp.log(den), lse[u])
    for u in range(t // qs):
        l_ref[u * qs:(u + 1) * qs, :] = lse[u]


def _dilated_group(qkv, g, seq):
    d, rows, _ = qkv.shape
    t, r = DIL_TILE, DIL_RADIUS
    w = DIL_HEADS * DIL_HD
    sub = t // r
    last = rows // r - 1

    def prev_blk(i):
        return jnp.maximum(i * sub - 1, 0)

    def next_blk(i):
        return jnp.minimum((i + 1) * sub, last)

    def cur(col):
        return pl.BlockSpec((None, t, w), lambda res, i: (res, i, col))

    def halo(col, blk):
        return pl.BlockSpec((None, r, w), lambda res, i: (res, blk(i), col))

    return pl.pallas_call(
        functools.partial(_dil_kernel, rows_per_seq=seq // d),
        out_shape=(jax.ShapeDtypeStruct((d, rows, w), F32), jax.ShapeDtypeStruct((d, rows, LANE), F32)),
        grid=(d, rows // t),
        in_specs=[cur(0), halo(1, prev_blk), cur(1), halo(1, next_blk), halo(2, prev_blk), cur(2), halo(2, next_blk)],
        out_specs=(pl.BlockSpec((None, t, w), lambda res, i: (res, i, 0)),
                   pl.BlockSpec((None, t, LANE), lambda res, i: (res, i, 0))),
        compiler_params=_params(("parallel", "arbitrary")),
        name=f"dilated_{g}",
    )(qkv, qkv, qkv, qkv, qkv, qkv, qkv)


def _dil_combine_kernel(o0, o1, o2, l0, l1, l2, y_ref, so1, so2, sl1, sl2):
    def token_order(ref, buf, lanes=slice(None)):
        d, rows, _ = ref.shape
        if d == 1:
            return ref[0, :, lanes]
        for r in range(d):
            buf[pl.ds(r, rows, stride=d), :] = ref[r, :, lanes]
        return buf[...]

    a, b, c = token_order(l0, None), token_order(l1, sl1), token_order(l2, sl2)
    mx = jnp.maximum(jnp.maximum(a, b), c)
    ea, eb, ec = jnp.exp(a - mx), jnp.exp(b - mx), jnp.exp(c - mx)
    inv = 1.0 / (ea + eb + ec)
    wa, wb, wc = ea * inv, eb * inv, ec * inv
    for h in range(DIL_HEADS):
        lanes = slice(h * DIL_HD, (h + 1) * DIL_HD)
        y = (wa[:, h:h + 1] * token_order(o0, None, lanes) + wb[:, h:h + 1] * token_order(o1, so1.at[h], lanes)
             + wc[:, h:h + 1] * token_order(o2, so2.at[h], lanes))
        y_ref[:, lanes] = y.astype(y_ref.dtype)


def _dil_combine(outs, lses, tm):
    w = outs[0].shape[2]
    m = outs[0].shape[0] * outs[0].shape[1]
    specs = [pl.BlockSpec((a.shape[0], tm // a.shape[0], a.shape[2]), lambda i: (0, i, 0)) for a in (*outs, *lses)]
    return pl.pallas_call(
        _dil_combine_kernel,
        out_shape=jax.ShapeDtypeStruct((m, w), BF16),
        grid=(m // tm,),
        in_specs=specs,
        out_specs=pl.BlockSpec((tm, w), lambda i: (i, 0)),
        scratch_shapes=[pltpu.VMEM((DIL_HEADS, tm, DIL_HD), F32)] * 2 + [pltpu.VMEM((tm, LANE), F32)] * 2,
        compiler_params=_params(("parallel",)),
        name="dilated_combine",
    )(*outs, *lses)


def _rope64(x, cos, sin_a, sin_b):
    return x * cos + pltpu.roll(x, 96, 1) * sin_a + pltpu.roll(x, 32, 1) * sin_b


def _mla_post_kernel(c_ref, gq_ref, gkv_ref, wuq_ref, wkn_ref, wv_ref, cos_ref, sa_ref, sb_ref,
                     q_ref, k_ref, v_ref):
    c = c_ref[...]
    cq = c[:, :MLA_QRANK]
    cq = cq * lax.rsqrt(jnp.mean(cq * cq, axis=-1, keepdims=True) + RMS_EPS) * gq_ref[...]
    ckv = c[:, MLA_QRANK:MLA_QRANK + MLA_KVRANK]
    ckv = (ckv * lax.rsqrt(jnp.mean(ckv * ckv, axis=-1, keepdims=True) + RMS_EPS) * gkv_ref[...]).astype(BF16)
    cos, sa, sb = cos_ref[...], sa_ref[...], sb_ref[...]
    scale = (MLA_NOPE + MLA_ROPE) ** -0.5
    q = jnp.dot(cq.astype(BF16), wuq_ref[...], preferred_element_type=F32) * scale
    kn = jnp.dot(ckv, wkn_ref[...], preferred_element_type=F32)
    kr = _rope64(c[:, MLA_QRANK + MLA_KVRANK:MLA_QRANK + MLA_KVRANK + LANE], cos, sa, sb).astype(BF16)
    for h in range(MLA_HEADS):
        base = h * MLA_HPAD
        q_ref[:, base:base + LANE] = q[:, base:base + LANE].astype(BF16)
        q_ref[:, base + LANE:base + 2 * LANE] = _rope64(q[:, base + LANE:base + 2 * LANE], cos, sa, sb).astype(BF16)
        k_ref[:, base:base + LANE] = kn[:, h * LANE:(h + 1) * LANE].astype(BF16)
        k_ref[:, base + LANE:base + 2 * LANE] = kr
    vv = jnp.dot(ckv, wv_ref[...], preferred_element_type=F32).astype(BF16)
    ones = jnp.ones((vv.shape[0], LANE), BF16)
    for h in range(MLA_HEADS):
        base = h * MLA_HPAD
        v_ref[:, base:base + LANE] = vv[:, h * LANE:(h + 1) * LANE]
        v_ref[:, base + LANE:base + 2 * LANE] = ones


def _mla_post(c, gq, gkv, wuq, wkn, wv, cos, sa, sb, seq, tm):
    m = c.shape[0]
    tiles_per_seq = seq // tm
    const = lambda i: (0, 0)
    tab = pl.BlockSpec((tm, LANE), lambda i: (i % tiles_per_seq, 0))
    hw = MLA_HEADS * MLA_HPAD
    return pl.pallas_call(
        _mla_post_kernel,
        out_shape=(jax.ShapeDtypeStruct((m, hw), BF16),) * 3,
        grid=(m // tm,),
        in_specs=[pl.BlockSpec((tm, C_COLS), lambda i: (i, 0)),
                  pl.BlockSpec((1, MLA_QRANK), const), pl.BlockSpec((1, MLA_KVRANK), const),
                  pl.BlockSpec(wuq.shape, const), pl.BlockSpec(wkn.shape, const), pl.BlockSpec(wv.shape, const),
                  tab, tab, tab],
        out_specs=(pl.BlockSpec((tm, hw), lambda i: (i, 0)),) * 3,
        compiler_params=_params(("parallel",)),
        name="mla_post",
    )(c, gq, gkv, wuq, wkn, wv, cos, sa, sb)


def _mla_attn_kernel(q_ref, k_ref, v_ref, o_ref, m_ref, acc_ref, s_ref, *, tk, sub):
    m_ref[...] = jnp.full_like(m_ref, -jnp.inf)
    acc_ref[...] = jnp.zeros_like(acc_ref)
    n_sub = q_ref.shape[0] // sub
    n_chunks = k_ref.shape[0] // tk
    nt = (((1,), (1,)), ((), ()))

    def chunk_rows(ci):
        if isinstance(ci, int):
            return pl.ds(ci * tk, tk)
        return pl.ds(pl.multiple_of(ci * tk, tk), tk)

    def step(slot, ci, has_next=True):
        if has_next:
            k_next = k_ref[chunk_rows(ci + 1), :]
        v_cur = v_ref[chunk_rows(ci), :]
        for r in range(n_sub):
            rs = slice(r * sub, (r + 1) * sub)
            if has_next:
                s_ref[1 - slot, rs, :] = lax.dot_general(q_ref[rs, :], k_next, nt, preferred_element_type=F32)
            s = s_ref[slot, rs, :]
            m_prev = m_ref[rs, :]
            m_new = jnp.maximum(m_prev, jnp.max(s, axis=-1, keepdims=True))
            alpha = jnp.exp(m_prev - m_new)
            p = jnp.exp(s - jnp.concatenate([m_new] * (tk // LANE), axis=1))
            acc_ref[rs, :] = (jnp.concatenate([alpha, alpha], axis=1) * acc_ref[rs, :]
                              + jnp.dot(p.astype(BF16), v_cur, preferred_element_type=F32))
            m_ref[rs, :] = m_new

    def body(i, carry):
        for c in range(MLA_CHUNKS_PER_TRIP):
            step(c % 2, MLA_CHUNKS_PER_TRIP * i + c)
        return carry

    k_first = k_ref[chunk_rows(0), :]
    for r in range(n_sub):
        rs = slice(r * sub, (r + 1) * sub)
        s_ref[0, rs, :] = lax.dot_general(q_ref[rs, :], k_first, nt, preferred_element_type=F32)
    n_trips = n_chunks // MLA_CHUNKS_PER_TRIP
    lax.fori_loop(0, n_trips - 1, body, 0)
    for c in range(MLA_CHUNKS_PER_TRIP):
        ci = MLA_CHUNKS_PER_TRIP * (n_trips - 1) + c
        step(c % 2, ci, has_next=ci + 1 < n_chunks)
    acc = acc_ref[...]
    o_ref[...] = (acc[:, :MLA_V] / acc[:, MLA_V:]).astype(o_ref.dtype)


def _mla_attn(q, k, v, seq, tq, tk, sub=256):
    m = q.shape[0]
    nseq = m // seq
    qt = seq // tq
    assert seq % (MLA_CHUNKS_PER_TRIP * tk) == 0 and tq % min(sub, tq) == 0
    return pl.pallas_call(
        functools.partial(_mla_attn_kernel, tk=tk, sub=min(sub, tq)),
        out_shape=jax.ShapeDtypeStruct((m, MLA_HEADS * MLA_V), BF16),
        grid=(nseq, MLA_HEADS, qt),
        in_specs=[pl.BlockSpec((tq, MLA_HPAD), lambda b, h, i: (b * qt + i, h)),
                  pl.BlockSpec((seq, MLA_HPAD), lambda b, h, i: (b, h)),
                  pl.BlockSpec((seq, MLA_HPAD), lambda b, h, i: (b, h))],
        out_specs=pl.BlockSpec((tq, MLA_V), lambda b, h, i: (b * qt + i, h)),
        scratch_shapes=[pltpu.VMEM((tq, LANE), F32), pltpu.VMEM((tq, MLA_HPAD), F32),
                        pltpu.VMEM((2, tq, tk), F32)],
        compiler_params=_params(("parallel", "parallel", "arbitrary")),
        name="mla_attn",
    )(q, k, v)


def _merge_kernel(x_ref, wg0, wg1, wg2, ya, yb, yc, wb0, wb1, wb2, o_ref):
    x = x_ref[...]
    acc = None
    for wg, y, wb in ((wg0, ya, wb0), (wg1, yb, wb1), (wg2, yc, wb2)):
        gate = _sigmoid(jnp.dot(x, wg[...], preferred_element_type=F32))
        term = gate * jnp.dot(y[...], wb[...], preferred_element_type=F32)
        acc = term if acc is None else acc + term
    o_ref[...] = acc.astype(o_ref.dtype)


def _merge(x, w_gate, ys, w_branch, tm, tn):
    m, k = x.shape
    n = w_branch.shape[2]
    nj = n // tn
    kb = w_branch.shape[1]
    gate_specs = [pl.BlockSpec((k, tn), functools.partial(lambda i, j, b: (0, b * nj + j), b=b)) for b in range(3)]
    y_specs = [pl.BlockSpec((tm, kb), lambda i, j: (i, 0))] * 3
    br_specs = [pl.BlockSpec((None, kb, tn), functools.partial(lambda i, j, b: (b, 0, j), b=b)) for b in range(3)]
    return pl.pallas_call(
        _merge_kernel,
        out_shape=jax.ShapeDtypeStruct((m, n), BF16),
        grid=(m // tm, nj),
        in_specs=[pl.BlockSpec((tm, k), lambda i, j: (i, 0))] + gate_specs + y_specs + br_specs,
        out_specs=pl.BlockSpec((tm, tn), lambda i, j: (i, j)),
        compiler_params=_params(("parallel", "arbitrary")),
        name="merge",
    )(x, w_gate, w_gate, w_gate, *ys, w_branch, w_branch, w_branch)


def _prep_stacks(w_out, ffn_g, ffn_u, ffn_d):
    fpad = D_FF_PAD - D_FF
    return dict(
        w_out=w_out.astype(BF16),
        ffn_g=jnp.pad(ffn_g, ((0, 0), (0, 0), (0, 0), (0, fpad))).astype(BF16),
        ffn_u=jnp.pad(ffn_u, ((0, 0), (0, 0), (0, 0), (0, fpad))).astype(BF16),
        ffn_d=jnp.pad(ffn_d, ((0, 0), (0, 0), (0, fpad), (0, 0))).astype(BF16),
    )


def _prep_layer(w_in, gate_w, gate_b, norm_g, q_norm_g, uq, kv_norm_g, ukv, w_branch, ln_g, ln_b):
    d = D_MODEL
    n_a = 2 * GLA_HEADS * GLA_DK + 2 * GLA_HEADS * GLA_DV
    n_rank = 2 * GLA_RANK
    n_b = len(DIL_DILATIONS) * 3 * BRANCH_W
    n_c = MLA_QRANK + MLA_KVRANK + MLA_ROPE
    o_b = n_a + n_rank
    o_c = o_b + n_b
    o_g = o_c + n_c
    w_c = jnp.concatenate([w_in[:, o_c:o_g], jnp.zeros((d, LANE - MLA_ROPE), F32),
                           w_in[:, n_a:o_b], jnp.zeros((d, LANE - n_rank), F32)], axis=1)
    gw = jnp.zeros((LANE, 2 * GLA_HEADS * GLA_DK), F32)
    gw = gw.at[:GLA_RANK, :GLA_HEADS * GLA_DK].set(gate_w[0])
    gw = gw.at[GLA_RANK:2 * GLA_RANK, GLA_HEADS * GLA_DK:].set(gate_w[1])
    uq_pad = jnp.pad(uq.reshape(MLA_QRANK, MLA_HEADS, MLA_NOPE + MLA_ROPE),
                     ((0, 0), (0, 0), (0, MLA_HPAD - MLA_NOPE - MLA_ROPE))).reshape(MLA_QRANK, MLA_HEADS * MLA_HPAD)
    ukv3 = ukv.reshape(MLA_KVRANK, MLA_HEADS, MLA_NOPE + MLA_V)
    return dict(
        w_a=w_in[:, :n_a].astype(BF16),
        w_dil=[w_in[:, o_b + g * 3 * BRANCH_W:o_b + (g + 1) * 3 * BRANCH_W].astype(BF16)
               for g in range(len(DIL_DILATIONS))],
        w_c=w_c.astype(BF16),
        w_gate=w_in[:, o_g:].astype(BF16),
        gla_gw=gw.astype(BF16),
        gla_gb=jnp.concatenate([gate_b[0], gate_b[1]])[None, :],
        gla_norm=norm_g[None, :],
        q_norm=q_norm_g[None, :],
        kv_norm=kv_norm_g[None, :],
        w_uq=uq_pad.astype(BF16),
        w_kn=ukv3[:, :, :MLA_NOPE].reshape(MLA_KVRANK, -1).astype(BF16),
        w_vv=ukv3[:, :, MLA_NOPE:].reshape(MLA_KVRANK, -1).astype(BF16),
        w_branch=w_branch.astype(BF16),
        ln_g=ln_g[:, None, :],
        ln_b=ln_b[:, None, :],
    )


def _rope_tables(seq):
    pos = jnp.arange(seq, dtype=F32)[:, None]
    half = DIL_HD // 2
    ang = pos * jnp.power(ROPE_THETA, -2.0 * jnp.arange(half, dtype=F32) / DIL_HD)[None, :]
    cos128 = jnp.concatenate([jnp.cos(ang), jnp.cos(ang)], axis=1)
    sin128 = jnp.concatenate([-jnp.sin(ang), jnp.sin(ang)], axis=1)
    half = MLA_ROPE // 2
    ang = pos * jnp.power(ROPE_THETA, -2.0 * jnp.arange(half, dtype=F32) / MLA_ROPE)[None, :]
    zero = jnp.zeros((seq, half), F32)
    pad = jnp.zeros((seq, LANE - MLA_ROPE), F32)
    cos64 = jnp.concatenate([jnp.cos(ang), jnp.cos(ang), pad], axis=1)
    sin_a = jnp.concatenate([-jnp.sin(ang), zero, pad], axis=1)
    sin_b = jnp.concatenate([zero, jnp.sin(ang), pad], axis=1)
    return cos128, sin128, cos64, sin_a, sin_b


def _layer(x, xb, l, p, stk, tabs, seq):
    cos128, sin128, cos64, sin_a, sin_b = tabs
    h = _ffn_up(xb, stk["ffn_g"], stk["ffn_u"], (l, 0), 1024, 512)
    x, xb = _mm_res_ln(h, stk["ffn_d"], (l, 0), x, p["ln_g"][0], p["ln_b"][0], 0.5, 512, 512, "ffn_down_ln")

    pa = _proj(xb, p["w_a"], F32, 1024, 512, "proj_gla")
    c = _proj(xb, p["w_c"], F32, 512, C_COLS, "proj_c")

    lg = _gla_gates(c, p["gla_gw"], p["gla_gb"], 512)
    y_a = _gla(pa, lg, p["gla_norm"], seq)

    outs, lses = [], []
    for g, d in enumerate(DIL_DILATIONS):
        qkv = _proj_dil(xb, p["w_dil"][g], cos128, sin128, seq, d, 1024, 1024, f"proj_dil_{g}")
        o, lse = _dilated_group(qkv, g, seq)
        outs.append(o)
        lses.append(lse)
    y_b = _dil_combine(outs, lses, 256)

    q_pad, k_pad, v_m = _mla_post(c, p["q_norm"], p["kv_norm"], p["w_uq"], p["w_kn"], p["w_vv"],
                                  cos64, sin_a, sin_b, seq, 256)
    y_c = _mla_attn(q_pad, k_pad, v_m, seq, 1024, 1024)

    merged = _merge(xb, p["w_gate"], (y_a, y_b, y_c), p["w_branch"], 512, 256)
    x, xb = _mm_res_ln(merged, stk["w_out"], (l,), x, p["ln_g"][1], p["ln_b"][1], 1.0, 512, 512, "out_proj_ln")

    h = _ffn_up(xb, stk["ffn_g"], stk["ffn_u"], (l, 1), 1024, 512)
    return _mm_res_ln(h, stk["ffn_d"], (l, 1), x, p["ln_g"][2], p["ln_b"][2], 0.5, 512, 512, "ffn_down_ln")


def _trunk(x3, layers, stk):
    bsz, seq, d = x3.shape
    x = x3.reshape(bsz * seq, d)
    xb = x.astype(BF16)
    tabs = _rope_tables(seq)
    for l, p in enumerate(layers):
        x, xb = _layer(x, xb, l, p, stk, tabs, seq)
    return x.reshape(bsz, seq, d)


def kernel(x_prompt, x_sample, w_in, gla_gate_w, gla_gate_b, gla_norm_g, mla_q_norm_g, mla_uq, mla_kv_norm_g,
           mla_ukv, w_branch, w_out, ffn_w_gate, ffn_w_up, ffn_w_down, ln_g, ln_b):
    layers = [_prep_layer(w_in[l], gla_gate_w[l], gla_gate_b[l], gla_norm_g[l], mla_q_norm_g[l], mla_uq[l],
                          mla_kv_norm_g[l], mla_ukv[l], w_branch[l], ln_g[l], ln_b[l]) for l in range(N_LAYERS)]
    stk = _prep_stacks(w_out, ffn_w_gate, ffn_w_up, ffn_w_down)
    return (_trunk(x_prompt, layers, stk), _trunk(x_sample, layers, stk))
```

```python
import functools
import math

import jax
import jax.numpy as jnp
from jax import lax
from jax.experimental import pallas as pl
from jax.experimental.pallas import tpu as pltpu

F32 = jnp.float32
BF16 = jnp.bfloat16

D_MODEL = 4096
N_LAYERS = 2
D_FF = 11008
D_FF_PAD = 11264
FFN_SUB_N = 256
ROPE_THETA = 10000.0
LN_EPS = 1e-5
RMS_EPS = 1e-6
NEG_INF = -1e30
DEEPNORM_ALPHA = (2 * N_LAYERS) ** 0.25
BRANCH_W = 2048

GLA_HEADS = 4
GLA_DK = 256
GLA_DV = 512
GLA_RANK = 16
GLA_NORMALIZER = 16.0
GLA_CHUNK = 64
GLA_SUB = 16
GLA_BLOCK = 256

DIL_DILATIONS = (1, 4, 16)
DIL_RADIUS = 64
DIL_HEADS = 16
DIL_HD = 128
DIL_TILE = 256
DIL_QSUB = 128
PERM_ROWS = 256

MLA_HEADS = 16
MLA_NOPE = 128
MLA_ROPE = 64
MLA_V = 128
MLA_QRANK = 896
MLA_KVRANK = 256
MLA_HPAD = 256
MLA_CHUNKS_PER_TRIP = 4
C_COLS = 1408
C_GATE_BLOCK = 10

LANE = 128
VMEM_LIMIT = 56 * 1024 * 1024
VMEM_LIMIT_WIDE = 60000 * 1024


def _params(sem, vmem=VMEM_LIMIT):
    return pltpu.CompilerParams(dimension_semantics=sem, vmem_limit_bytes=vmem)


def _sigmoid(x):
    return 1.0 / (1.0 + jnp.exp(-x))


def _proj_kernel(x_ref, w_ref, o_ref):
    o_ref[...] = jnp.dot(x_ref[...], w_ref[...], preferred_element_type=F32).astype(o_ref.dtype)


def _proj(x, w, out_dtype, tm, tn, name):
    m, k = x.shape
    n = w.shape[1]
    return pl.pallas_call(
        _proj_kernel,
        out_shape=jax.ShapeDtypeStruct((m, n), out_dtype),
        grid=(m // tm, n // tn),
        in_specs=[pl.BlockSpec((tm, k), lambda i, j: (i, 0)),
                  pl.BlockSpec((k, tn), lambda i, j: (0, j))],
        out_specs=pl.BlockSpec((tm, tn), lambda i, j: (i, j)),
        compiler_params=_params(("parallel", "arbitrary")),
        name=name,
    )(x, w)


def _proj_dil_kernel(x_ref, w_ref, cos_ref, sin_ref, o_ref, *, d, n_q_tiles, n_rope_tiles, q_scale, sub_n):
    j = pl.program_id(1)
    x = x_ref[...]
    is_rope = j < n_rope_tiles
    scale = jnp.where(j < n_q_tiles, q_scale, 1.0).astype(F32)
    cos = jnp.where(is_rope, cos_ref[...] * scale, 1.0)
    sin = jnp.where(is_rope, sin_ref[...] * scale, 0.0)
    if d > 1:
        per = PERM_ROWS // d
        out_row = lax.broadcasted_iota(jnp.int32, (PERM_ROWS, PERM_ROWS), 0)
        in_row = lax.broadcasted_iota(jnp.int32, (PERM_ROWS, PERM_ROWS), 1)
        src = (out_row & (per - 1)) * d + lax.shift_right_logical(out_row, per.bit_length() - 1)
        perm = jnp.where(in_row == src, 1.0, 0.0).astype(BF16)
    for t in range(w_ref.shape[1] // sub_n):
        cols = slice(t * sub_n, (t + 1) * sub_n)
        acc = jnp.dot(x, w_ref[:, cols], preferred_element_type=F32)
        heads = []
        for h in range(sub_n // LANE):
            y = acc[:, h * LANE:(h + 1) * LANE]
            heads.append((y * cos + pltpu.roll(y, LANE // 2, 1) * sin).astype(BF16))
        res = jnp.concatenate(heads, axis=1)
        if d == 1:
            o_ref[0, :, cols] = res
            continue
        for b in range(res.shape[0] // PERM_ROWS):
            moved = jnp.dot(perm, res[b * PERM_ROWS:(b + 1) * PERM_ROWS, :],
                            preferred_element_type=F32).astype(o_ref.dtype)
            for r in range(d):
                o_ref[r, b * per:(b + 1) * per, cols] = moved[r * per:(r + 1) * per, :]


def _proj_dil(x, w, cos, sin, seq, d, tm, tn, name, sub_n=512):
    m, k = x.shape
    n = w.shape[1]
    tiles_per_seq = seq // tm
    w_head = DIL_HEADS * DIL_HD
    return pl.pallas_call(
        functools.partial(_proj_dil_kernel, d=d, n_q_tiles=w_head // tn, n_rope_tiles=2 * w_head // tn,
                          q_scale=DIL_HD ** -0.5, sub_n=min(sub_n, tn)),
        out_shape=jax.ShapeDtypeStruct((d, m // d, n), BF16),
        grid=(m // tm, n // tn),
        in_specs=[pl.BlockSpec((tm, k), lambda i, j: (i, 0)),
                  pl.BlockSpec((k, tn), lambda i, j: (0, j)),
                  pl.BlockSpec((tm, LANE), lambda i, j: (i % tiles_per_seq, 0)),
                  pl.BlockSpec((tm, LANE), lambda i, j: (i % tiles_per_seq, 0))],
        out_specs=pl.BlockSpec((d, tm // d, tn), lambda i, j: (0, i, j)),
        compiler_params=_params(("parallel", "arbitrary")),
        name=name,
    )(x, w, cos, sin)


def _ffn_up_kernel(x_ref, wg_ref, wu_ref, o_ref, *, n_valid):
    x = x_ref[...]
    col0 = pl.program_id(1) * o_ref.shape[1]
    for t in range(o_ref.shape[1] // FFN_SUB_N):
        cols = slice(t * FFN_SUB_N, (t + 1) * FFN_SUB_N)
        g = jnp.dot(x, wg_ref[:, cols].astype(BF16), preferred_element_type=F32)
        u = jnp.dot(x, wu_ref[:, cols].astype(BF16), preferred_element_type=F32)
        h = jnp.where(col0 + t * FFN_SUB_N < n_valid, g * _sigmoid(g) * u, 0.0)
        o_ref[:, cols] = h.astype(o_ref.dtype)


def _stacked_spec(lead, block, index_map):
    return pl.BlockSpec((None,) * len(lead) + tuple(block), lambda *g: tuple(lead) + tuple(index_map(*g)))


def _ffn_up(x, wg, wu, lead, n_out, tm, tn):
    m, k = x.shape
    n = wg.shape[-1]
    assert n % FFN_SUB_N == 0 and n_out % tn == 0 and n_out - n < tn
    w_spec = _stacked_spec(lead, (k, tn), lambda i, j: (0, j))
    return pl.pallas_call(
        functools.partial(_ffn_up_kernel, n_valid=n),
        out_shape=jax.ShapeDtypeStruct((m, n_out), BF16),
        grid=(m // tm, n_out // tn),
        in_specs=[pl.BlockSpec((tm, k), lambda i, j: (i, 0)), w_spec, w_spec],
        out_specs=pl.BlockSpec((tm, tn), lambda i, j: (i, j)),
        compiler_params=_params(("parallel", "arbitrary"), VMEM_LIMIT_WIDE),
        name="ffn_up",
    )(x, wg, wu)


def _mm_res_ln_kernel(a_ref, w_ref, x_ref, g_ref, b_ref, of_ref, ob_ref, *, scale):
    kk = pl.program_id(1)

    @pl.when(kk == 0)
    def _():
        of_ref[...] = jnp.zeros_like(of_ref)

    of_ref[...] += jnp.dot(a_ref[...], w_ref[...], preferred_element_type=F32)

    @pl.when(kk == pl.num_programs(1) - 1)
    def _():
        z = DEEPNORM_ALPHA * x_ref[...] + scale * of_ref[...]
        mu = jnp.mean(z, axis=-1, keepdims=True)
        zc = z - mu
        var = jnp.mean(zc * zc, axis=-1, keepdims=True)
        y = zc * lax.rsqrt(var + LN_EPS) * g_ref[...] + b_ref[...]
        of_ref[...] = y
        ob_ref[...] = y.astype(BF16)


def _mm_res_ln(a, w, lead, x, g, b, scale, tm, tk, name):
    m, k = a.shape
    n = w.shape[-1]
    return pl.pallas_call(
        functools.partial(_mm_res_ln_kernel, scale=scale),
        out_shape=(jax.ShapeDtypeStruct((m, n), F32), jax.ShapeDtypeStruct((m, n), BF16)),
        grid=(m // tm, k // tk),
        in_specs=[pl.BlockSpec((tm, tk), lambda i, kk: (i, kk)),
                  _stacked_spec(lead, (tk, n), lambda i, kk: (kk, 0)),
                  pl.BlockSpec((tm, n), lambda i, kk: (i, 0)),
                  pl.BlockSpec((1, n), lambda i, kk: (0, 0)),
                  pl.BlockSpec((1, n), lambda i, kk: (0, 0))],
        out_specs=(pl.BlockSpec((tm, n), lambda i, kk: (i, 0)),
                   pl.BlockSpec((tm, n), lambda i, kk: (i, 0))),
        compiler_params=_params(("parallel", "arbitrary"), VMEM_LIMIT_WIDE),
        name=name,
    )(a, w, x, g, b)


def _gla_gate_kernel(c_ref, gw_ref, gb_ref, o_ref):
    z = jnp.dot(c_ref[...].astype(BF16), gw_ref[...], preferred_element_type=F32) + gb_ref[...]
    log_sig = jnp.minimum(z, 0.0) - jnp.log(1.0 + jnp.exp(-jnp.abs(z)))
    o_ref[...] = log_sig * (1.0 / GLA_NORMALIZER)


def _gla_gates(c, gw, gb, tm):
    m = c.shape[0]
    n = gw.shape[1]
    return pl.pallas_call(
        _gla_gate_kernel,
        out_shape=jax.ShapeDtypeStruct((m, n), F32),
        grid=(m // tm,),
        in_specs=[pl.BlockSpec((tm, LANE), lambda i: (i, C_GATE_BLOCK)),
                  pl.BlockSpec((LANE, n), lambda i: (0, 0)),
                  pl.BlockSpec((1, n), lambda i: (0, 0))],
        out_specs=pl.BlockSpec((tm, n), lambda i: (i, 0)),
        compiler_params=_params(("parallel",)),
        name="gla_gates",
    )(c, gw, gb)


def _gla_chunk(q, k, v, lg, st_ref, reverse):
    c, sub = GLA_CHUNK, GLA_SUB
    nsub = c // sub
    row = lax.broadcasted_iota(jnp.int32, (c, c), 0)
    col = lax.broadcasted_iota(jnp.int32, (c, c), 1)
    tri = jnp.where((col >= row) if reverse else (col <= row), 1.0, 0.0).astype(BF16)
    hi = lg.astype(BF16)
    r1 = lg - hi.astype(F32)
    mid = r1.astype(BF16)
    lo = (r1 - mid.astype(F32)).astype(BF16)
    b = (jnp.dot(tri, hi, preferred_element_type=F32) + jnp.dot(tri, mid, preferred_element_type=F32)
         + jnp.dot(tri, lo, preferred_element_type=F32))
    b = b * math.log2(math.e)
    total = b[0:1, :] if reverse else b[c - 1:c, :]

    q = q * (GLA_DK ** -0.5)
    st = st_ref[...]
    o = lax.dot_general((q * jnp.exp2(b)).astype(BF16), st.astype(BF16), (((1,), (1,)), ((), ())),
                        preferred_element_type=F32)
    k_state = (k * jnp.exp2(total - b)).astype(BF16)
    st_ref[...] = st * jnp.exp2(total) + lax.dot_general(
        v.astype(BF16), k_state, (((0,), (0,)), ((), ())), preferred_element_type=F32)

    lane = lax.broadcasted_iota(jnp.int32, (sub, c), 1)
    lrow = lax.broadcasted_iota(jnp.int32, (sub, c), 0)
    blocks = []
    for i_sub in range(nsub):
        r0 = i_sub * sub
        q_r = q[r0:r0 + sub, :]
        b_r = b[r0:r0 + sub, :]
        a_blk = jnp.zeros((sub, c), F32)
        has_off = (i_sub < nsub - 1) if reverse else (i_sub > 0)
        if has_off:
            edge = r0 + sub if reverse else r0 - 1
            ref = b[edge:edge + 1, :]
            q_t = (q_r * jnp.exp2(b_r - ref)).astype(BF16)
            k_t = (k * jnp.exp2(jnp.minimum(ref - b, 0.0))).astype(BF16)
            a_off = lax.dot_general(q_t, k_t, (((1,), (1,)), ((), ())), preferred_element_type=F32)
            keep = (lane >= r0 + sub) if reverse else (lane < r0)
            a_blk = jnp.where(keep, a_off, 0.0)
        for jj in range(sub):
            j = r0 + jj
            w = q_r * k[j:j + 1, :] * jnp.exp2(jnp.minimum(b_r - b[j:j + 1, :], 0.0))
            s = jnp.sum(w, axis=1, keepdims=True)
            keep = (lrow < jj) if reverse else (lrow >= jj)
            a_blk = a_blk + jnp.where((lane == j) & keep, s, 0.0)
        blocks.append(a_blk)
    a = jnp.concatenate(blocks, axis=0)
    return o + jnp.dot(a.astype(BF16), v.astype(BF16), preferred_element_type=F32)


def _gla_fwd_kernel(q_ref, k_ref, v_ref, lg_ref, o_ref, st_ref, *, blocks_per_seq):
    @pl.when(pl.program_id(1) % blocks_per_seq == 0)
    def _():
        st_ref[...] = jnp.zeros_like(st_ref)

    def body(ci, carry):
        rows = pl.ds(pl.multiple_of(ci * GLA_CHUNK, GLA_CHUNK), GLA_CHUNK)
        o_ref[rows, :] = _gla_chunk(q_ref[rows, :], k_ref[rows, :], v_ref[rows, :], lg_ref[rows, :],
                                    st_ref, False)
        return carry

    lax.fori_loop(0, GLA_BLOCK // GLA_CHUNK, body, 0, unroll=True)


def _gla_bwd_kernel(q_ref, k_ref, v_ref, lg_ref, of_ref, r_ref, g_ref, y_ref, st_ref, *, blocks_per_seq):
    @pl.when(pl.program_id(1) % blocks_per_seq == 0)
    def _():
        st_ref[...] = jnp.zeros_like(st_ref)

    n_chunks = GLA_BLOCK // GLA_CHUNK

    def body(ci, carry):
        rows = pl.ds(pl.multiple_of((n_chunks - 1 - ci) * GLA_CHUNK, GLA_CHUNK), GLA_CHUNK)
        o = of_ref[rows, :] + _gla_chunk(q_ref[rows, :], k_ref[rows, :], v_ref[rows, :], lg_ref[rows, :],
                                         st_ref, True)
        o = o * lax.rsqrt(jnp.mean(o * o, axis=-1, keepdims=True) + RMS_EPS) * g_ref[...]
        r = r_ref[rows, :]
        y_ref[rows, :] = (o * (r * _sigmoid(r))).astype(y_ref.dtype)
        return carry

    lax.fori_loop(0, n_chunks, body, 0, unroll=True)


def _gla(pa, lg, norm_g, seq):
    m = pa.shape[0]
    nb = m // GLA_BLOCK
    bps = seq // GLA_BLOCK
    hk = GLA_HEADS
    blk = GLA_BLOCK

    def specs(row_of):
        return [pl.BlockSpec((blk, GLA_DK), lambda h, i: (row_of(i), h)),
                pl.BlockSpec((blk, GLA_DK), lambda h, i: (row_of(i), hk + h)),
                pl.BlockSpec((blk, GLA_DV), lambda h, i: (row_of(i), hk + h))]

    fwd_row = lambda i: i
    bwd_row = lambda i: nb - 1 - i
    scratch = [pltpu.VMEM((GLA_DV, GLA_DK), F32)]
    o_f = pl.pallas_call(
        functools.partial(_gla_fwd_kernel, blocks_per_seq=bps),
        out_shape=jax.ShapeDtypeStruct((m, GLA_HEADS * GLA_DV), F32),
        grid=(GLA_HEADS, nb),
        in_specs=specs(fwd_row) + [pl.BlockSpec((blk, GLA_DK), lambda h, i: (i, h))],
        out_specs=pl.BlockSpec((blk, GLA_DV), lambda h, i: (i, h)),
        scratch_shapes=scratch,
        compiler_params=_params(("parallel", "arbitrary")),
        name="gla_fwd",
    )(pa, pa, pa, lg)
    return pl.pallas_call(
        functools.partial(_gla_bwd_kernel, blocks_per_seq=bps),
        out_shape=jax.ShapeDtypeStruct((m, GLA_HEADS * GLA_DV), BF16),
        grid=(GLA_HEADS, nb),
        in_specs=specs(bwd_row) + [
            pl.BlockSpec((blk, GLA_DK), lambda h, i: (bwd_row(i), hk + h)),
            pl.BlockSpec((blk, GLA_DV), lambda h, i: (bwd_row(i), h)),
            pl.BlockSpec((blk, GLA_DV), lambda h, i: (bwd_row(i), 2 * hk + h)),
            pl.BlockSpec((1, GLA_DV), lambda h, i: (0, 0))],
        out_specs=pl.BlockSpec((blk, GLA_DV), lambda h, i: (bwd_row(i), h)),
        scratch_shapes=scratch,
        compiler_params=_params(("parallel", "arbitrary")),
        name="gla_bwd",
    )(pa, pa, pa, lg, o_f, pa, norm_g)


def _dil_kernel(q_ref, kp_ref, kc_ref, kn_ref, vp_ref, vc_ref, vn_ref, o_ref, l_ref, *, rows_per_seq):
    t, r, qs = DIL_TILE, DIL_RADIUS, DIL_QSUB
    start = pl.program_id(1) * t
    seq_lo = (start // rows_per_seq) * rows_per_seq
    row = lax.broadcasted_iota(jnp.int32, (qs, qs + 2 * r), 0)
    col = lax.broadcasted_iota(jnp.int32, (qs, qs + 2 * r), 1)
    rel = col - row
    in_band = (rel >= 0) & (rel <= 2 * r)
    valid = []
    for u in range(t // qs):
        pos = start + u * qs - r + col
        valid.append(in_band & (pos >= seq_lo) & (pos < seq_lo + rows_per_seq))
    head_lane = lax.broadcasted_iota(jnp.int32, (qs, LANE), 1)
    lse = [jnp.zeros((qs, LANE), F32) for _ in range(t // qs)]
    for h in range(DIL_HEADS):
        sl = slice(h * DIL_HD, (h + 1) * DIL_HD)
        kw = jnp.concatenate([kp_ref[:, sl], kc_ref[:, sl], kn_ref[:, sl]], axis=0)
        vw = jnp.concatenate([vp_ref[:, sl], vc_ref[:, sl], vn_ref[:, sl]], axis=0)
        for u in range(t // qs):
            rows = slice(u * qs, (u + 1) * qs)
            win = slice(u * qs, (u + 1) * qs + 2 * r)
            s = lax.dot_general(q_ref[rows, sl], kw[win], (((1,), (1,)), ((), ())), preferred_element_type=F32)
            s = jnp.where(valid[u], s, NEG_INF)
            m = jnp.max(s, axis=-1, keepdims=True)
            p = jnp.exp(s - m)
            den = jnp.sum(p, axis=-1, keepdims=True)
            o_ref[rows, sl] = jnp.dot(p.astype(BF16), vw[win], preferred_element_type=F32) / den
            lse[u] = jnp.where(head_lane == h, m + jnp.log(den), lse[u])
    for u in range(t // qs):
        l_ref[u * qs:(u + 1) * qs, :] = lse[u]


def _dilated_group(qkv, g, seq):
    d, rows, _ = qkv.shape
    t, r = DIL_TILE, DIL_RADIUS
    w = DIL_HEADS * DIL_HD
    sub = t // r
    last = rows // r - 1

    def prev_blk(i):
        return jnp.maximum(i * sub - 1, 0)

    def next_blk(i):
        return jnp.minimum((i + 1) * sub, last)

    def cur(col):
        return pl.BlockSpec((None, t, w), lambda res, i: (res, i, col))

    def halo(col, blk):
        return pl.BlockSpec((None, r, w), lambda res, i: (res, blk(i), col))

    return pl.pallas_call(
        functools.partial(_dil_kernel, rows_per_seq=seq // d),
        out_shape=(jax.ShapeDtypeStruct((d, rows, w), F32), jax.ShapeDtypeStruct((d, rows, LANE), F32)),
        grid=(d, rows // t),
        in_specs=[cur(0), halo(1, prev_blk), cur(1), halo(1, next_blk), halo(2, prev_blk), cur(2), halo(2, next_blk)],
        out_specs=(pl.BlockSpec((None, t, w), lambda res, i: (res, i, 0)),
                   pl.BlockSpec((None, t, LANE), lambda res, i: (res, i, 0))),
        compiler_params=_params(("parallel", "arbitrary")),
        name=f"dilated_{g}",
    )(qkv, qkv, qkv, qkv, qkv, qkv, qkv)


def _dil_combine_kernel(o0, o1, o2, l0, l1, l2, y_ref, so1, so2, sl1, sl2):
    def token_order(ref, buf, lanes=slice(None)):
        d, rows, _ = ref.shape
        if d == 1:
            return ref[0, :, lanes]
        for r in range(d):
            buf[pl.ds(r, rows, stride=d), :] = ref[r, :, lanes]
        return buf[...]

    a, b, c = token_order(l0, None), token_order(l1, sl1), token_order(l2, sl2)
    mx = jnp.maximum(jnp.maximum(a, b), c)
    ea, eb, ec = jnp.exp(a - mx), jnp.exp(b - mx), jnp.exp(c - mx)
    inv = 1.0 / (ea + eb + ec)
    wa, wb, wc = ea * inv, eb * inv, ec * inv
    for h in range(DIL_HEADS):
        lanes = slice(h * DIL_HD, (h + 1) * DIL_HD)
        y = (wa[:, h:h + 1] * token_order(o0, None, lanes) + wb[:, h:h + 1] * token_order(o1, so1.at[h], lanes)
             + wc[:, h:h + 1] * token_order(o2, so2.at[h], lanes))
        y_ref[:, lanes] = y.astype(y_ref.dtype)


def _dil_combine(outs, lses, tm):
    w = outs[0].shape[2]
    m = outs[0].shape[0] * outs[0].shape[1]
    specs = [pl.BlockSpec((a.shape[0], tm // a.shape[0], a.shape[2]), lambda i: (0, i, 0)) for a in (*outs, *lses)]
    return pl.pallas_call(
        _dil_combine_kernel,
        out_shape=jax.ShapeDtypeStruct((m, w), BF16),
        grid=(m // tm,),
        in_specs=specs,
        out_specs=pl.BlockSpec((tm, w), lambda i: (i, 0)),
        scratch_shapes=[pltpu.VMEM((DIL_HEADS, tm, DIL_HD), F32)] * 2 + [pltpu.VMEM((tm, LANE), F32)] * 2,
        compiler_params=_params(("parallel",)),
        name="dilated_combine",
    )(*outs, *lses)


def _rope64(x, cos, sin_a, sin_b):
    return x * cos + pltpu.roll(x, 96, 1) * sin_a + pltpu.roll(x, 32, 1) * sin_b


def _mla_post_kernel(c_ref, gq_ref, gkv_ref, wuq_ref, wkn_ref, wv_ref, cos_ref, sa_ref, sb_ref,
                     q_ref, k_ref, v_ref):
    c = c_ref[...]
    cq = c[:, :MLA_QRANK]
    cq = cq * lax.rsqrt(jnp.mean(cq * cq, axis=-1, keepdims=True) + RMS_EPS) * gq_ref[...]
    ckv = c[:, MLA_QRANK:MLA_QRANK + MLA_KVRANK]
    ckv = (ckv * lax.rsqrt(jnp.mean(ckv * ckv, axis=-1, keepdims=True) + RMS_EPS) * gkv_ref[...]).astype(BF16)
    cos, sa, sb = cos_ref[...], sa_ref[...], sb_ref[...]
    scale = (MLA_NOPE + MLA_ROPE) ** -0.5
    q = jnp.dot(cq.astype(BF16), wuq_ref[...], preferred_element_type=F32) * scale
    kn = jnp.dot(ckv, wkn_ref[...], preferred_element_type=F32)
    kr = _rope64(c[:, MLA_QRANK + MLA_KVRANK:MLA_QRANK + MLA_KVRANK + LANE], cos, sa, sb).astype(BF16)
    for h in range(MLA_HEADS):
        base = h * MLA_HPAD
        q_ref[:, base:base + LANE] = q[:, base:base + LANE].astype(BF16)
        q_ref[:, base + LANE:base + 2 * LANE] = _rope64(q[:, base + LANE:base + 2 * LANE], cos, sa, sb).astype(BF16)
        k_ref[:, base:base + LANE] = kn[:, h * LANE:(h + 1) * LANE].astype(BF16)
        k_ref[:, base + LANE:base + 2 * LANE] = kr
    vv = jnp.dot(ckv, wv_ref[...], preferred_element_type=F32).astype(BF16)
    ones = jnp.ones((vv.shape[0], LANE), BF16)
    for h in range(MLA_HEADS):
        base = h * MLA_HPAD
        v_ref[:, base:base + LANE] = vv[:, h * LANE:(h + 1) * LANE]
        v_ref[:, base + LANE:base + 2 * LANE] = ones


def _mla_post(c, gq, gkv, wuq, wkn, wv, cos, sa, sb, seq, tm):
    m = c.shape[0]
    tiles_per_seq = seq // tm
    const = lambda i: (0, 0)
    tab = pl.BlockSpec((tm, LANE), lambda i: (i % tiles_per_seq, 0))
    hw = MLA_HEADS * MLA_HPAD
    return pl.pallas_call(
        _mla_post_kernel,
        out_shape=(jax.ShapeDtypeStruct((m, hw), BF16),) * 3,
        grid=(m // tm,),
        in_specs=[pl.BlockSpec((tm, C_COLS), lambda i: (i, 0)),
                  pl.BlockSpec((1, MLA_QRANK), const), pl.BlockSpec((1, MLA_KVRANK), const),
                  pl.BlockSpec(wuq.shape, const), pl.BlockSpec(wkn.shape, const), pl.BlockSpec(wv.shape, const),
                  tab, tab, tab],
        out_specs=(pl.BlockSpec((tm, hw), lambda i: (i, 0)),) * 3,
        compiler_params=_params(("parallel",)),
        name="mla_post",
    )(c, gq, gkv, wuq, wkn, wv, cos, sa, sb)


def _mla_attn_kernel(q_ref, k_ref, v_ref, o_ref, m_ref, acc_ref, s_ref, *, tk, sub):
    m_ref[...] = jnp.full_like(m_ref, -jnp.inf)
    acc_ref[...] = jnp.zeros_like(acc_ref)
    n_sub = q_ref.shape[0] // sub
    n_chunks = k_ref.shape[0] // tk
    nt = (((1,), (1,)), ((), ()))

    def chunk_rows(ci):
        if isinstance(ci, int):
            return pl.ds(ci * tk, tk)
        return pl.ds(pl.multiple_of(ci * tk, tk), tk)

    def step(slot, ci, has_next=True):
        if has_next:
            k_next = k_ref[chunk_rows(ci + 1), :]
        v_cur = v_ref[chunk_rows(ci), :]
        for r in range(n_sub):
            rs = slice(r * sub, (r + 1) * sub)
            if has_next:
                s_ref[1 - slot, rs, :] = lax.dot_general(q_ref[rs, :], k_next, nt, preferred_element_type=F32)
            s = s_ref[slot, rs, :]
            m_prev = m_ref[rs, :]
            m_new = jnp.maximum(m_prev, jnp.max(s, axis=-1, keepdims=True))
            alpha = jnp.exp(m_prev - m_new)
            p = jnp.exp(s - jnp.concatenate([m_new] * (tk // LANE), axis=1))
            acc_ref[rs, :] = (jnp.concatenate([alpha, alpha], axis=1) * acc_ref[rs, :]
                              + jnp.dot(p.astype(BF16), v_cur, preferred_element_type=F32))
            m_ref[rs, :] = m_new

    def body(i, carry):
        for c in range(MLA_CHUNKS_PER_TRIP):
            step(c % 2, MLA_CHUNKS_PER_TRIP * i + c)
        return carry

    k_first = k_ref[chunk_rows(0), :]
    for r in range(n_sub):
        rs = slice(r * sub, (r + 1) * sub)
        s_ref[0, rs, :] = lax.dot_general(q_ref[rs, :], k_first, nt, preferred_element_type=F32)
    n_trips = n_chunks // MLA_CHUNKS_PER_TRIP
    lax.fori_loop(0, n_trips - 1, body, 0)
    for c in range(MLA_CHUNKS_PER_TRIP):
        ci = MLA_CHUNKS_PER_TRIP * (n_trips - 1) + c
        step(c % 2, ci, has_next=ci + 1 < n_chunks)
    acc = acc_ref[...]
    o_ref[...] = (acc[:, :MLA_V] / acc[:, MLA_V:]).astype(o_ref.dtype)


def _mla_attn(q, k, v, seq, tq, tk, sub=256):
    m = q.shape[0]
    nseq = m // seq
    qt = seq // tq
    assert seq % (MLA_CHUNKS_PER_TRIP * tk) == 0 and tq % min(sub, tq) == 0
    return pl.pallas_call(
        functools.partial(_mla_attn_kernel, tk=tk, sub=min(sub, tq)),
        out_shape=jax.ShapeDtypeStruct((m, MLA_HEADS * MLA_V), BF16),
        grid=(nseq, MLA_HEADS, qt),
        in_specs=[pl.BlockSpec((tq, MLA_HPAD), lambda b, h, i: (b * qt + i, h)),
                  pl.BlockSpec((seq, MLA_HPAD), lambda b, h, i: (b, h)),
                  pl.BlockSpec((seq, MLA_HPAD), lambda b, h, i: (b, h))],
        out_specs=pl.BlockSpec((tq, MLA_V), lambda b, h, i: (b * qt + i, h)),
        scratch_shapes=[pltpu.VMEM((tq, LANE), F32), pltpu.VMEM((tq, MLA_HPAD), F32),
                        pltpu.VMEM((2, tq, tk), F32)],
        compiler_params=_params(("parallel", "parallel", "arbitrary")),
        name="mla_attn",
    )(q, k, v)


def _merge_kernel(x_ref, wg0, wg1, wg2, ya, yb, yc, wb0, wb1, wb2, o_ref):
    x = x_ref[...]
    acc = None
    for wg, y, wb in ((wg0, ya, wb0), (wg1, yb, wb1), (wg2, yc, wb2)):
        gate = _sigmoid(jnp.dot(x, wg[...], preferred_element_type=F32))
        term = gate * jnp.dot(y[...], wb[...], preferred_element_type=F32)
        acc = term if acc is None else acc + term
    o_ref[...] = acc.astype(o_ref.dtype)


def _merge(x, w_gate, ys, w_branch, tm, tn):
    m, k = x.shape
    n = w_branch.shape[2]
    nj = n // tn
    kb = w_branch.shape[1]
    gate_specs = [pl.BlockSpec((k, tn), functools.partial(lambda i, j, b: (0, b * nj + j), b=b)) for b in range(3)]
    y_specs = [pl.BlockSpec((tm, kb), lambda i, j: (i, 0))] * 3
    br_specs = [pl.BlockSpec((None, kb, tn), functools.partial(lambda i, j, b: (b, 0, j), b=b)) for b in range(3)]
    return pl.pallas_call(
        _merge_kernel,
        out_shape=jax.ShapeDtypeStruct((m, n), BF16),
        grid=(m // tm, nj),
        in_specs=[pl.BlockSpec((tm, k), lambda i, j: (i, 0))] + gate_specs + y_specs + br_specs,
        out_specs=pl.BlockSpec((tm, tn), lambda i, j: (i, j)),
        compiler_params=_params(("parallel", "arbitrary")),
        name="merge",
    )(x, w_gate, w_gate, w_gate, *ys, w_branch, w_branch, w_branch)


def _prep_stacks(w_out, ffn_g, ffn_u, ffn_d):
    fpad = D_FF_PAD - D_FF
    return dict(
        w_out=w_out.astype(BF16),
        ffn_g=ffn_g,
        ffn_u=ffn_u,
        ffn_d=jnp.pad(ffn_d, ((0, 0), (0, 0), (0, fpad), (0, 0))).astype(BF16),
    )


def _prep_layer(w_in, gate_w, gate_b, norm_g, q_norm_g, uq, kv_norm_g, ukv, w_branch, ln_g, ln_b):
    d = D_MODEL
    n_a = 2 * GLA_HEADS * GLA_DK + 2 * GLA_HEADS * GLA_DV
    n_rank = 2 * GLA_RANK
    n_b = len(DIL_DILATIONS) * 3 * BRANCH_W
    n_c = MLA_QRANK + MLA_KVRANK + MLA_ROPE
    o_b = n_a + n_rank
    o_c = o_b + n_b
    o_g = o_c + n_c
    w_c = jnp.concatenate([w_in[:, o_c:o_g], jnp.zeros((d, LANE - MLA_ROPE), F32),
                           w_in[:, n_a:o_b], jnp.zeros((d, LANE - n_rank), F32)], axis=1)
    gw = jnp.zeros((LANE, 2 * GLA_HEADS * GLA_DK), F32)
    gw = gw.at[:GLA_RANK, :GLA_HEADS * GLA_DK].set(gate_w[0])
    gw = gw.at[GLA_RANK:2 * GLA_RANK, GLA_HEADS * GLA_DK:].set(gate_w[1])
    uq_pad = jnp.pad(uq.reshape(MLA_QRANK, MLA_HEADS, MLA_NOPE + MLA_ROPE),
                     ((0, 0), (0, 0), (0, MLA_HPAD - MLA_NOPE - MLA_ROPE))).reshape(MLA_QRANK, MLA_HEADS * MLA_HPAD)
    ukv3 = ukv.reshape(MLA_KVRANK, MLA_HEADS, MLA_NOPE + MLA_V)
    return dict(
        w_a=w_in[:, :n_a].astype(BF16),
        w_dil=[w_in[:, o_b + g * 3 * BRANCH_W:o_b + (g + 1) * 3 * BRANCH_W].astype(BF16)
               for g in range(len(DIL_DILATIONS))],
        w_c=w_c.astype(BF16),
        w_gate=w_in[:, o_g:].astype(BF16),
        gla_gw=gw.astype(BF16),
        gla_gb=jnp.concatenate([gate_b[0], gate_b[1]])[None, :],
        gla_norm=norm_g[None, :],
        q_norm=q_norm_g[None, :],
        kv_norm=kv_norm_g[None, :],
        w_uq=uq_pad.astype(BF16),
        w_kn=ukv3[:, :, :MLA_NOPE].reshape(MLA_KVRANK, -1).astype(BF16),
        w_vv=ukv3[:, :, MLA_NOPE:].reshape(MLA_KVRANK, -1).astype(BF16),
        w_branch=w_branch.astype(BF16),
        ln_g=ln_g[:, None, :],
        ln_b=ln_b[:, None, :],
    )


def _rope_tables(seq):
    pos = jnp.arange(seq, dtype=F32)[:, None]
    half = DIL_HD // 2
    ang = pos * jnp.power(ROPE_THETA, -2.0 * jnp.arange(half, dtype=F32) / DIL_HD)[None, :]
    cos128 = jnp.concatenate([jnp.cos(ang), jnp.cos(ang)], axis=1)
    sin128 = jnp.concatenate([-jnp.sin(ang), jnp.sin(ang)], axis=1)
    half = MLA_ROPE // 2
    ang = pos * jnp.power(ROPE_THETA, -2.0 * jnp.arange(half, dtype=F32) / MLA_ROPE)[None, :]
    zero = jnp.zeros((seq, half), F32)
    pad = jnp.zeros((seq, LANE - MLA_ROPE), F32)
    cos64 = jnp.concatenate([jnp.cos(ang), jnp.cos(ang), pad], axis=1)
    sin_a = jnp.concatenate([-jnp.sin(ang), zero, pad], axis=1)
    sin_b = jnp.concatenate([zero, jnp.sin(ang), pad], axis=1)
    return cos128, sin128, cos64, sin_a, sin_b


def _layer(x, xb, l, p, stk, tabs, seq):
    cos128, sin128, cos64, sin_a, sin_b = tabs
    h = _ffn_up(xb, stk["ffn_g"], stk["ffn_u"], (l, 0), D_FF_PAD, 1024, 512)
    x, xb = _mm_res_ln(h, stk["ffn_d"], (l, 0), x, p["ln_g"][0], p["ln_b"][0], 0.5, 512, 512, "ffn_down_ln")

    pa = _proj(xb, p["w_a"], F32, 1024, 512, "proj_gla")
    c = _proj(xb, p["w_c"], F32, 512, C_COLS, "proj_c")

    lg = _gla_gates(c, p["gla_gw"], p["gla_gb"], 512)
    y_a = _gla(pa, lg, p["gla_norm"], seq)

    outs, lses = [], []
    for g, d in enumerate(DIL_DILATIONS):
        qkv = _proj_dil(xb, p["w_dil"][g], cos128, sin128, seq, d, 1024, 1024, f"proj_dil_{g}")
        o, lse = _dilated_group(qkv, g, seq)
        outs.append(o)
        lses.append(lse)
    y_b = _dil_combine(outs, lses, 256)

    q_pad, k_pad, v_m = _mla_post(c, p["q_norm"], p["kv_norm"], p["w_uq"], p["w_kn"], p["w_vv"],
                                  cos64, sin_a, sin_b, seq, 256)
    y_c = _mla_attn(q_pad, k_pad, v_m, seq, 1024, 1024)

    merged = _merge(xb, p["w_gate"], (y_a, y_b, y_c), p["w_branch"], 512, 256)
    x, xb = _mm_res_ln(merged, stk["w_out"], (l,), x, p["ln_g"][1], p["ln_b"][1], 1.0, 512, 512, "out_proj_ln")

    h = _ffn_up(xb, stk["ffn_g"], stk["ffn_u"], (l, 1), D_FF_PAD, 1024, 512)
    return _mm_res_ln(h, stk["ffn_d"], (l, 1), x, p["ln_g"][2], p["ln_b"][2], 0.5, 512, 512, "ffn_down_ln")


def _trunk(x3, layers, stk):
    bsz, seq, d = x3.shape
    x = x3.reshape(bsz * seq, d)
    xb = x.astype(BF16)
    tabs = _rope_tables(seq)
    for l, p in enumerate(layers):
        x, xb = _layer(x, xb, l, p, stk, tabs, seq)
    return x.reshape(bsz, seq, d)


def kernel(x_prompt, x_sample, w_in, gla_gate_w, gla_gate_b, gla_norm_g, mla_q_norm_g, mla_uq, mla_kv_norm_g,
           mla_ukv, w_branch, w_out, ffn_w_gate, ffn_w_up, ffn_w_down, ln_g, ln_b):
    layers = [_prep_layer(w_in[l], gla_gate_w[l], gla_gate_b[l], gla_norm_g[l], mla_q_norm_g[l], mla_uq[l],
                          mla_kv_norm_g[l], mla_ukv[l], w_branch[l], ln_g[l], ln_b[l]) for l in range(N_LAYERS)]
    stk = _prep_stacks(w_out, ffn_w_gate, ffn_w_up, ffn_w_down)
    return (_trunk(x_prompt, layers, stk), _trunk(x_sample, layers, stk))
```

```python
import functools
import math

import jax
import jax.numpy as jnp
from jax import lax
from jax.experimental import pallas as pl
from jax.experimental.pallas import tpu as pltpu

F32 = jnp.float32
BF16 = jnp.bfloat16

D_MODEL = 4096
N_LAYERS = 2
D_FF = 11008
D_FF_PAD = 11264
FFN_SUB_N = 256
ROPE_THETA = 10000.0
LN_EPS = 1e-5
RMS_EPS = 1e-6
NEG_INF = -1e30
DEEPNORM_ALPHA = (2 * N_LAYERS) ** 0.25
BRANCH_W = 2048

GLA_HEADS = 4
GLA_DK = 256
GLA_DV = 512
GLA_RANK = 16
GLA_NORMALIZER = 16.0
GLA_CHUNK = 64
GLA_SUB = 16
GLA_BLOCK = 256

DIL_DILATIONS = (1, 4, 16)
DIL_RADIUS = 64
DIL_HEADS = 16
DIL_HD = 128
DIL_TILE = 256
DIL_QSUB = 128
PERM_ROWS = 256

MLA_HEADS = 16
MLA_NOPE = 128
MLA_ROPE = 64
MLA_V = 128
MLA_QRANK = 896
MLA_KVRANK = 256
MLA_HPAD = 256
MLA_CHUNKS_PER_TRIP = 4
C_COLS = 1408
C_GATE_BLOCK = 10

LN_ROWS = 16
LANE = 128
VMEM_LIMIT = 56 * 1024 * 1024
VMEM_LIMIT_WIDE = 60000 * 1024


def _params(sem, vmem=VMEM_LIMIT):
    return pltpu.CompilerParams(dimension_semantics=sem, vmem_limit_bytes=vmem)


def _sigmoid(x):
    return 1.0 / (1.0 + jnp.exp(-x))


def _proj_kernel(x_ref, w_ref, o_ref):
    o_ref[...] = jnp.dot(x_ref[...], w_ref[...], preferred_element_type=F32).astype(o_ref.dtype)


def _proj(x, w, out_dtype, tm, tn, name):
    m, k = x.shape
    n = w.shape[1]
    return pl.pallas_call(
        _proj_kernel,
        out_shape=jax.ShapeDtypeStruct((m, n), out_dtype),
        grid=(m // tm, n // tn),
        in_specs=[pl.BlockSpec((tm, k), lambda i, j: (i, 0)),
                  pl.BlockSpec((k, tn), lambda i, j: (0, j))],
        out_specs=pl.BlockSpec((tm, tn), lambda i, j: (i, j)),
        compiler_params=_params(("parallel", "arbitrary")),
        name=name,
    )(x, w)


def _proj_dil_kernel(x_ref, w_ref, cos_ref, sin_ref, o_ref, *, d, n_q_tiles, n_rope_tiles, q_scale, sub_n):
    j = pl.program_id(1)
    x = x_ref[...]
    is_rope = j < n_rope_tiles
    scale = jnp.where(j < n_q_tiles, q_scale, 1.0).astype(F32)
    cos = jnp.where(is_rope, cos_ref[...] * scale, 1.0)
    sin = jnp.where(is_rope, sin_ref[...] * scale, 0.0)
    if d > 1:
        per = PERM_ROWS // d
        out_row = lax.broadcasted_iota(jnp.int32, (PERM_ROWS, PERM_ROWS), 0)
        in_row = lax.broadcasted_iota(jnp.int32, (PERM_ROWS, PERM_ROWS), 1)
        src = (out_row & (per - 1)) * d + lax.shift_right_logical(out_row, per.bit_length() - 1)
        perm = jnp.where(in_row == src, 1.0, 0.0).astype(BF16)
    for t in range(w_ref.shape[1] // sub_n):
        cols = slice(t * sub_n, (t + 1) * sub_n)
        acc = jnp.dot(x, w_ref[:, cols], preferred_element_type=F32)
        heads = []
        for h in range(sub_n // LANE):
            y = acc[:, h * LANE:(h + 1) * LANE]
            heads.append((y * cos + pltpu.roll(y, LANE // 2, 1) * sin).astype(BF16))
        res = jnp.concatenate(heads, axis=1)
        if d == 1:
            o_ref[0, :, cols] = res
            continue
        for b in range(res.shape[0] // PERM_ROWS):
            moved = jnp.dot(perm, res[b * PERM_ROWS:(b + 1) * PERM_ROWS, :],
                            preferred_element_type=F32).astype(o_ref.dtype)
            for r in range(d):
                o_ref[r, b * per:(b + 1) * per, cols] = moved[r * per:(r + 1) * per, :]


def _proj_dil(x, w, cos, sin, seq, d, tm, tn, name, sub_n=512):
    m, k = x.shape
    n = w.shape[1]
    tiles_per_seq = seq // tm
    w_head = DIL_HEADS * DIL_HD
    return pl.pallas_call(
        functools.partial(_proj_dil_kernel, d=d, n_q_tiles=w_head // tn, n_rope_tiles=2 * w_head // tn,
                          q_scale=DIL_HD ** -0.5, sub_n=min(sub_n, tn)),
        out_shape=jax.ShapeDtypeStruct((d, m // d, n), BF16),
        grid=(m // tm, n // tn),
        in_specs=[pl.BlockSpec((tm, k), lambda i, j: (i, 0)),
                  pl.BlockSpec((k, tn), lambda i, j: (0, j)),
                  pl.BlockSpec((tm, LANE), lambda i, j: (i % tiles_per_seq, 0)),
                  pl.BlockSpec((tm, LANE), lambda i, j: (i % tiles_per_seq, 0))],
        out_specs=pl.BlockSpec((d, tm // d, tn), lambda i, j: (0, i, j)),
        compiler_params=_params(("parallel", "arbitrary")),
        name=name,
    )(x, w, cos, sin)


def _ffn_up_kernel(x_ref, wg_ref, wu_ref, o_ref, *, n_valid):
    x = x_ref[...]
    col0 = pl.program_id(1) * o_ref.shape[1]
    for t in range(o_ref.shape[1] // FFN_SUB_N):
        cols = slice(t * FFN_SUB_N, (t + 1) * FFN_SUB_N)
        g = jnp.dot(x, wg_ref[:, cols].astype(BF16), preferred_element_type=F32)
        u = jnp.dot(x, wu_ref[:, cols].astype(BF16), preferred_element_type=F32)
        h = jnp.where(col0 + t * FFN_SUB_N < n_valid, g * _sigmoid(g) * u, 0.0)
        o_ref[:, cols] = h.astype(o_ref.dtype)


def _stacked_spec(lead, block, index_map):
    return pl.BlockSpec((None,) * len(lead) + tuple(block), lambda *g: tuple(lead) + tuple(index_map(*g)))


def _ffn_up(x, wg, wu, lead, n_out, tm, tn):
    m, k = x.shape
    n = wg.shape[-1]
    assert n % FFN_SUB_N == 0 and n_out % tn == 0 and n_out - n < tn
    w_spec = _stacked_spec(lead, (k, tn), lambda i, j: (0, j))
    return pl.pallas_call(
        functools.partial(_ffn_up_kernel, n_valid=n),
        out_shape=jax.ShapeDtypeStruct((m, n_out), BF16),
        grid=(m // tm, n_out // tn),
        in_specs=[pl.BlockSpec((tm, k), lambda i, j: (i, 0)), w_spec, w_spec],
        out_specs=pl.BlockSpec((tm, tn), lambda i, j: (i, j)),
        compiler_params=_params(("parallel", "arbitrary"), VMEM_LIMIT_WIDE),
        name="ffn_up",
    )(x, wg, wu)


def _mm_res_ln_kernel(a_ref, w_ref, x_ref, g_ref, b_ref, of_ref, ob_ref, acc_ref, *, scale):
    kk = pl.program_id(1)
    half = pl.program_id(2)

    @pl.when(kk == 0)
    def _():
        acc_ref[half] = jnp.zeros(acc_ref.shape[1:], F32)

    acc_ref[half] += jnp.dot(a_ref[...], w_ref[...], preferred_element_type=F32)

    @pl.when(kk == pl.num_programs(1) - 1)
    def _():
        def norm_rows(ri, carry):
            rows = pl.ds(pl.multiple_of(ri * LN_ROWS, LN_ROWS), LN_ROWS)
            z = DEEPNORM_ALPHA * x_ref[rows, :] + scale * acc_ref[half, rows, :]
            mu = jnp.mean(z, axis=-1, keepdims=True)
            zc = z - mu
            var = jnp.mean(zc * zc, axis=-1, keepdims=True)
            y = zc * lax.rsqrt(var + LN_EPS) * g_ref[...] + b_ref[...]
            of_ref[rows, :] = y
            ob_ref[rows, :] = y.astype(BF16)
            return carry

        lax.fori_loop(0, of_ref.shape[0] // LN_ROWS, norm_rows, 0, unroll=2)


def _mm_res_ln(a, w, lead, x, g, b, scale, tm, tk, name):
    m, k = a.shape
    n = w.shape[-1]
    last = k // tk - 1

    def row_block(i2, kk, half):
        return (2 * i2 + jnp.where(kk == last, half, 0), 0)

    return pl.pallas_call(
        functools.partial(_mm_res_ln_kernel, scale=scale),
        out_shape=(jax.ShapeDtypeStruct((m, n), F32), jax.ShapeDtypeStruct((m, n), BF16)),
        grid=(m // (2 * tm), k // tk, 2),
        in_specs=[pl.BlockSpec((tm, tk), lambda i2, kk, half: (2 * i2 + half, kk)),
                  _stacked_spec(lead, (tk, n), lambda i2, kk, half: (kk, 0)),
                  pl.BlockSpec((tm, n), row_block, pipeline_mode=pl.Buffered(1)),
                  pl.BlockSpec((1, n), lambda i2, kk, half: (0, 0)),
                  pl.BlockSpec((1, n), lambda i2, kk, half: (0, 0))],
        out_specs=(pl.BlockSpec((tm, n), row_block), pl.BlockSpec((tm, n), row_block)),
        scratch_shapes=[pltpu.VMEM((2, tm, n), F32)],
        compiler_params=_params(("parallel", "arbitrary", "arbitrary"), VMEM_LIMIT_WIDE),
        name=name,
    )(a, w, x, g, b)


def _gla_gate_kernel(c_ref, gw_ref, gb_ref, o_ref):
    z = jnp.dot(c_ref[...].astype(BF16), gw_ref[...], preferred_element_type=F32) + gb_ref[...]
    log_sig = jnp.minimum(z, 0.0) - jnp.log(1.0 + jnp.exp(-jnp.abs(z)))
    o_ref[...] = log_sig * (1.0 / GLA_NORMALIZER)


def _gla_gates(c, gw, gb, tm):
    m = c.shape[0]
    n = gw.shape[1]
    return pl.pallas_call(
        _gla_gate_kernel,
        out_shape=jax.ShapeDtypeStruct((m, n), F32),
        grid=(m // tm,),
        in_specs=[pl.BlockSpec((tm, LANE), lambda i: (i, C_GATE_BLOCK)),
                  pl.BlockSpec((LANE, n), lambda i: (0, 0)),
                  pl.BlockSpec((1, n), lambda i: (0, 0))],
        out_specs=pl.BlockSpec((tm, n), lambda i: (i, 0)),
        compiler_params=_params(("parallel",)),
        name="gla_gates",
    )(c, gw, gb)


def _gla_chunk(q, k, v, lg, st_ref, reverse):
    c, sub = GLA_CHUNK, GLA_SUB
    nsub = c // sub
    row = lax.broadcasted_iota(jnp.int32, (c, c), 0)
    col = lax.broadcasted_iota(jnp.int32, (c, c), 1)
    tri = jnp.where((col >= row) if reverse else (col <= row), 1.0, 0.0).astype(BF16)
    hi = lg.astype(BF16)
    r1 = lg - hi.astype(F32)
    mid = r1.astype(BF16)
    lo = (r1 - mid.astype(F32)).astype(BF16)
    b = (jnp.dot(tri, hi, preferred_element_type=F32) + jnp.dot(tri, mid, preferred_element_type=F32)
         + jnp.dot(tri, lo, preferred_element_type=F32))
    b = b * math.log2(math.e)
    total = b[0:1, :] if reverse else b[c - 1:c, :]

    q = q * (GLA_DK ** -0.5)
    st = st_ref[...]
    o = lax.dot_general((q * jnp.exp2(b)).astype(BF16), st.astype(BF16), (((1,), (1,)), ((), ())),
                        preferred_element_type=F32)
    k_state = (k * jnp.exp2(total - b)).astype(BF16)
    st_ref[...] = st * jnp.exp2(total) + lax.dot_general(
        v.astype(BF16), k_state, (((0,), (0,)), ((), ())), preferred_element_type=F32)

    lane = lax.broadcasted_iota(jnp.int32, (sub, c), 1)
    lrow = lax.broadcasted_iota(jnp.int32, (sub, c), 0)
    blocks = []
    for i_sub in range(nsub):
        r0 = i_sub * sub
        q_r = q[r0:r0 + sub, :]
        b_r = b[r0:r0 + sub, :]
        a_blk = jnp.zeros((sub, c), F32)
        has_off = (i_sub < nsub - 1) if reverse else (i_sub > 0)
        if has_off:
            edge = r0 + sub if reverse else r0 - 1
            ref = b[edge:edge + 1, :]
            q_t = (q_r * jnp.exp2(b_r - ref)).astype(BF16)
            k_t = (k * jnp.exp2(jnp.minimum(ref - b, 0.0))).astype(BF16)
            a_off = lax.dot_general(q_t, k_t, (((1,), (1,)), ((), ())), preferred_element_type=F32)
            keep = (lane >= r0 + sub) if reverse else (lane < r0)
            a_blk = jnp.where(keep, a_off, 0.0)
        for jj in range(sub):
            j = r0 + jj
            w = q_r * k[j:j + 1, :] * jnp.exp2(jnp.minimum(b_r - b[j:j + 1, :], 0.0))
            s = jnp.sum(w, axis=1, keepdims=True)
            keep = (lrow < jj) if reverse else (lrow >= jj)
            a_blk = a_blk + jnp.where((lane == j) & keep, s, 0.0)
        blocks.append(a_blk)
    a = jnp.concatenate(blocks, axis=0)
    return o + jnp.dot(a.astype(BF16), v.astype(BF16), preferred_element_type=F32)


def _gla_fwd_kernel(q_ref, k_ref, v_ref, lg_ref, o_ref, st_ref, *, blocks_per_seq):
    @pl.when(pl.program_id(1) % blocks_per_seq == 0)
    def _():
        st_ref[...] = jnp.zeros_like(st_ref)

    def body(ci, carry):
        rows = pl.ds(pl.multiple_of(ci * GLA_CHUNK, GLA_CHUNK), GLA_CHUNK)
        o_ref[rows, :] = _gla_chunk(q_ref[rows, :], k_ref[rows, :], v_ref[rows, :], lg_ref[rows, :],
                                    st_ref, False)
        return carry

    lax.fori_loop(0, GLA_BLOCK // GLA_CHUNK, body, 0, unroll=True)


def _gla_bwd_kernel(q_ref, k_ref, v_ref, lg_ref, of_ref, r_ref, g_ref, y_ref, st_ref, *, blocks_per_seq):
    @pl.when(pl.program_id(1) % blocks_per_seq == 0)
    def _():
        st_ref[...] = jnp.zeros_like(st_ref)

    n_chunks = GLA_BLOCK // GLA_CHUNK

    def body(ci, carry):
        rows = pl.ds(pl.multiple_of((n_chunks - 1 - ci) * GLA_CHUNK, GLA_CHUNK), GLA_CHUNK)
        o = of_ref[rows, :] + _gla_chunk(q_ref[rows, :], k_ref[rows, :], v_ref[rows, :], lg_ref[rows, :],
                                         st_ref, True)
        o = o * lax.rsqrt(jnp.mean(o * o, axis=-1, keepdims=True) + RMS_EPS) * g_ref[...]
        r = r_ref[rows, :]
        y_ref[rows, :] = (o * (r * _sigmoid(r))).astype(y_ref.dtype)
        return carry

    lax.fori_loop(0, n_chunks, body, 0, unroll=True)


def _gla(pa, lg, norm_g, seq):
    m = pa.shape[0]
    nb = m // GLA_BLOCK
    bps = seq // GLA_BLOCK
    hk = GLA_HEADS
    blk = GLA_BLOCK

    def specs(row_of):
        return [pl.BlockSpec((blk, GLA_DK), lambda h, i: (row_of(i), h)),
                pl.BlockSpec((blk, GLA_DK), lambda h, i: (row_of(i), hk + h)),
                pl.BlockSpec((blk, GLA_DV), lambda h, i: (row_of(i), hk + h))]

    fwd_row = lambda i: i
    bwd_row = lambda i: nb - 1 - i
    scratch = [pltpu.VMEM((GLA_DV, GLA_DK), F32)]
    o_f = pl.pallas_call(
        functools.partial(_gla_fwd_kernel, blocks_per_seq=bps),
        out_shape=jax.ShapeDtypeStruct((m, GLA_HEADS * GLA_DV), F32),
        grid=(GLA_HEADS, nb),
        in_specs=specs(fwd_row) + [pl.BlockSpec((blk, GLA_DK), lambda h, i: (i, h))],
        out_specs=pl.BlockSpec((blk, GLA_DV), lambda h, i: (i, h)),
        scratch_shapes=scratch,
        compiler_params=_params(("parallel", "arbitrary")),
        name="gla_fwd",
    )(pa, pa, pa, lg)
    return pl.pallas_call(
        functools.partial(_gla_bwd_kernel, blocks_per_seq=bps),
        out_shape=jax.ShapeDtypeStruct((m, GLA_HEADS * GLA_DV), BF16),
        grid=(GLA_HEADS, nb),
        in_specs=specs(bwd_row) + [
            pl.BlockSpec((blk, GLA_DK), lambda h, i: (bwd_row(i), hk + h)),
            pl.BlockSpec((blk, GLA_DV), lambda h, i: (bwd_row(i), h)),
            pl.BlockSpec((blk, GLA_DV), lambda h, i: (bwd_row(i), 2 * hk + h)),
            pl.BlockSpec((1, GLA_DV), lambda h, i: (0, 0))],
        out_specs=pl.BlockSpec((blk, GLA_DV), lambda h, i: (bwd_row(i), h)),
        scratch_shapes=scratch,
        compiler_params=_params(("parallel", "arbitrary")),
        name="gla_bwd",
    )(pa, pa, pa, lg, o_f, pa, norm_g)


def _dil_kernel(q_ref, kp_ref, kc_ref, kn_ref, vp_ref, vc_ref, vn_ref, o_ref, l_ref, *, rows_per_seq):
    t, r, qs = DIL_TILE, DIL_RADIUS, DIL_QSUB
    start = pl.program_id(1) * t
    seq_lo = (start // rows_per_seq) * rows_per_seq
    row = lax.broadcasted_iota(jnp.int32, (qs, qs + 2 * r), 0)
    col = lax.broadcasted_iota(jnp.int32, (qs, qs + 2 * r), 1)
    rel = col - row
    in_band = (rel >= 0) & (rel <= 2 * r)
    valid = []
    for u in range(t // qs):
        pos = start + u * qs - r + col
        valid.append(in_band & (pos >= seq_lo) & (pos < seq_lo + rows_per_seq))
    head_lane = lax.broadcasted_iota(jnp.int32, (qs, LANE), 1)
    lse = [jnp.zeros((qs, LANE), F32) for _ in range(t // qs)]
    for h in range(DIL_HEADS):
        sl = slice(h * DIL_HD, (h + 1) * DIL_HD)
        kw = jnp.concatenate([kp_ref[:, sl], kc_ref[:, sl], kn_ref[:, sl]], axis=0)
        vw = jnp.concatenate([vp_ref[:, sl], vc_ref[:, sl], vn_ref[:, sl]], axis=0)
        for u in range(t // qs):
            rows = slice(u * qs, (u + 1) * qs)
            win = slice(u * qs, (u + 1) * qs + 2 * r)
            s = lax.dot_general(q_ref[rows, sl], kw[win], (((1,), (1,)), ((), ())), preferred_element_type=F32)
            s = jnp.where(valid[u], s, NEG_INF)
            m = jnp.max(s, axis=-1, keepdims=True)
            p = jnp.exp(s - m)
            den = jnp.sum(p, axis=-1, keepdims=True)
            o_ref[rows, sl] = jnp.dot(p.astype(BF16), vw[win], preferred_element_type=F32) / den
            lse[u] = jnp.where(head_lane == h, m + jnp.log(den), lse[u])
    for u in range(t // qs):
        l_ref[u * qs:(u + 1) * qs, :] = lse[u]


def _dilated_group(qkv, g, seq):
    d, rows, _ = qkv.shape
    t, r = DIL_TILE, DIL_RADIUS
    w = DIL_HEADS * DIL_HD
    sub = t // r
    last = rows // r - 1

    def prev_blk(i):
        return jnp.maximum(i * sub - 1, 0)

    def next_blk(i):
        return jnp.minimum((i + 1) * sub, last)

    def cur(col):
        return pl.BlockSpec((None, t, w), lambda res, i: (res, i, col))

    def halo(col, blk):
        return pl.BlockSpec((None, r, w), lambda res, i: (res, blk(i), col))

    return pl.pallas_call(
        functools.partial(_dil_kernel, rows_per_seq=seq // d),
        out_shape=(jax.ShapeDtypeStruct((d, rows, w), F32), jax.ShapeDtypeStruct((d, rows, LANE), F32)),
        grid=(d, rows // t),
        in_specs=[cur(0), halo(1, prev_blk), cur(1), halo(1, next_blk), halo(2, prev_blk), cur(2), halo(2, next_blk)],
        out_specs=(pl.BlockSpec((None, t, w), lambda res, i: (res, i, 0)),
                   pl.BlockSpec((None, t, LANE), lambda res, i: (res, i, 0))),
        compiler_params=_params(("parallel", "arbitrary")),
        name=f"dilated_{g}",
    )(qkv, qkv, qkv, qkv, qkv, qkv, qkv)


def _dil_combine_kernel(o0, o1, o2, l0, l1, l2, y_ref, so1, so2, sl1, sl2):
    def token_order(ref, buf, lanes=slice(None)):
        d, rows, _ = ref.shape
        if d == 1:
            return ref[0, :, lanes]
        for r in range(d):
            buf[pl.ds(r, rows, stride=d), :] = ref[r, :, lanes]
        return buf[...]

    a, b, c = token_order(l0, None), token_order(l1, sl1), token_order(l2, sl2)
    mx = jnp.maximum(jnp.maximum(a, b), c)
    ea, eb, ec = jnp.exp(a - mx), jnp.exp(b - mx), jnp.exp(c - mx)
    inv = 1.0 / (ea + eb + ec)
    wa, wb, wc = ea * inv, eb * inv, ec * inv
    for h in range(DIL_HEADS):
        lanes = slice(h * DIL_HD, (h + 1) * DIL_HD)
        y = (wa[:, h:h + 1] * token_order(o0, None, lanes) + wb[:, h:h + 1] * token_order(o1, so1.at[h], lanes)
             + wc[:, h:h + 1] * token_order(o2, so2.at[h], lanes))
        y_ref[:, lanes] = y.astype(y_ref.dtype)


def _dil_combine(outs, lses, tm):
    w = outs[0].shape[2]
    m = outs[0].shape[0] * outs[0].shape[1]
    specs = [pl.BlockSpec((a.shape[0], tm // a.shape[0], a.shape[2]), lambda i: (0, i, 0)) for a in (*outs, *lses)]
    return pl.pallas_call(
        _dil_combine_kernel,
        out_shape=jax.ShapeDtypeStruct((m, w), BF16),
        grid=(m // tm,),
        in_specs=specs,
        out_specs=pl.BlockSpec((tm, w), lambda i: (i, 0)),
        scratch_shapes=[pltpu.VMEM((DIL_HEADS, tm, DIL_HD), F32)] * 2 + [pltpu.VMEM((tm, LANE), F32)] * 2,
        compiler_params=_params(("parallel",)),
        name="dilated_combine",
    )(*outs, *lses)


def _rope64(x, cos, sin_a, sin_b):
    return x * cos + pltpu.roll(x, 96, 1) * sin_a + pltpu.roll(x, 32, 1) * sin_b


def _mla_post_kernel(c_ref, gq_ref, gkv_ref, wuq_ref, wkn_ref, wv_ref, cos_ref, sa_ref, sb_ref,
                     q_ref, k_ref, v_ref):
    c = c_ref[...]
    cq = c[:, :MLA_QRANK]
    cq = cq * lax.rsqrt(jnp.mean(cq * cq, axis=-1, keepdims=True) + RMS_EPS) * gq_ref[...]
    ckv = c[:, MLA_QRANK:MLA_QRANK + MLA_KVRANK]
    ckv = (ckv * lax.rsqrt(jnp.mean(ckv * ckv, axis=-1, keepdims=True) + RMS_EPS) * gkv_ref[...]).astype(BF16)
    cos, sa, sb = cos_ref[...], sa_ref[...], sb_ref[...]
    scale = (MLA_NOPE + MLA_ROPE) ** -0.5
    q = jnp.dot(cq.astype(BF16), wuq_ref[...], preferred_element_type=F32) * scale
    kn = jnp.dot(ckv, wkn_ref[...], preferred_element_type=F32)
    kr = _rope64(c[:, MLA_QRANK + MLA_KVRANK:MLA_QRANK + MLA_KVRANK + LANE], cos, sa, sb).astype(BF16)
    for h in range(MLA_HEADS):
        base = h * MLA_HPAD
        q_ref[:, base:base + LANE] = q[:, base:base + LANE].astype(BF16)
        q_ref[:, base + LANE:base + 2 * LANE] = _rope64(q[:, base + LANE:base + 2 * LANE], cos, sa, sb).astype(BF16)
        k_ref[:, base:base + LANE] = kn[:, h * LANE:(h + 1) * LANE].astype(BF16)
        k_ref[:, base + LANE:base + 2 * LANE] = kr
    vv = jnp.dot(ckv, wv_ref[...], preferred_element_type=F32).astype(BF16)
    ones = jnp.ones((vv.shape[0], LANE), BF16)
    for h in range(MLA_HEADS):
        base = h * MLA_HPAD
        v_ref[:, base:base + LANE] = vv[:, h * LANE:(h + 1) * LANE]
        v_ref[:, base + LANE:base + 2 * LANE] = ones


def _mla_post(c, gq, gkv, wuq, wkn, wv, cos, sa, sb, seq, tm):
    m = c.shape[0]
    tiles_per_seq = seq // tm
    const = lambda i: (0, 0)
    tab = pl.BlockSpec((tm, LANE), lambda i: (i % tiles_per_seq, 0))
    hw = MLA_HEADS * MLA_HPAD
    return pl.pallas_call(
        _mla_post_kernel,
        out_shape=(jax.ShapeDtypeStruct((m, hw), BF16),) * 3,
        grid=(m // tm,),
        in_specs=[pl.BlockSpec((tm, C_COLS), lambda i: (i, 0)),
                  pl.BlockSpec((1, MLA_QRANK), const), pl.BlockSpec((1, MLA_KVRANK), const),
                  pl.BlockSpec(wuq.shape, const), pl.BlockSpec(wkn.shape, const), pl.BlockSpec(wv.shape, const),
                  tab, tab, tab],
        out_specs=(pl.BlockSpec((tm, hw), lambda i: (i, 0)),) * 3,
        compiler_params=_params(("parallel",)),
        name="mla_post",
    )(c, gq, gkv, wuq, wkn, wv, cos, sa, sb)


def _mla_attn_kernel(q_ref, k_ref, v_ref, o_ref, m_ref, acc_ref, s_ref, *, tk, sub):
    m_ref[...] = jnp.full_like(m_ref, -jnp.inf)
    acc_ref[...] = jnp.zeros_like(acc_ref)
    n_sub = q_ref.shape[0] // sub
    n_chunks = k_ref.shape[0] // tk
    nt = (((1,), (1,)), ((), ()))

    def chunk_rows(ci):
        if isinstance(ci, int):
            return pl.ds(ci * tk, tk)
        return pl.ds(pl.multiple_of(ci * tk, tk), tk)

    def step(slot, ci, has_next=True):
        if has_next:
            k_next = k_ref[chunk_rows(ci + 1), :]
        v_cur = v_ref[chunk_rows(ci), :]
        for r in range(n_sub):
            rs = slice(r * sub, (r + 1) * sub)
            if has_next:
                s_ref[1 - slot, rs, :] = lax.dot_general(q_ref[rs, :], k_next, nt, preferred_element_type=F32)
            s = s_ref[slot, rs, :]
            m_prev = m_ref[rs, :]
            m_new = jnp.maximum(m_prev, jnp.max(s, axis=-1, keepdims=True))
            alpha = jnp.exp(m_prev - m_new)
            p = jnp.exp(s - jnp.concatenate([m_new] * (tk // LANE), axis=1))
            acc_ref[rs, :] = (jnp.concatenate([alpha, alpha], axis=1) * acc_ref[rs, :]
                              + jnp.dot(p.astype(BF16), v_cur, preferred_element_type=F32))
            m_ref[rs, :] = m_new

    def body(i, carry):
        for c in range(MLA_CHUNKS_PER_TRIP):
            step(c % 2, MLA_CHUNKS_PER_TRIP * i + c)
        return carry

    k_first = k_ref[chunk_rows(0), :]
    for r in range(n_sub):
        rs = slice(r * sub, (r + 1) * sub)
        s_ref[0, rs, :] = lax.dot_general(q_ref[rs, :], k_first, nt, preferred_element_type=F32)
    n_trips = n_chunks // MLA_CHUNKS_PER_TRIP
    lax.fori_loop(0, n_trips - 1, body, 0)
    for c in range(MLA_CHUNKS_PER_TRIP):
        ci = MLA_CHUNKS_PER_TRIP * (n_trips - 1) + c
        step(c % 2, ci, has_next=ci + 1 < n_chunks)
    acc = acc_ref[...]
    o_ref[...] = (acc[:, :MLA_V] / acc[:, MLA_V:]).astype(o_ref.dtype)


def _mla_attn(q, k, v, seq, tq, tk, sub=256):
    m = q.shape[0]
    nseq = m // seq
    qt = seq // tq
    assert seq % (MLA_CHUNKS_PER_TRIP * tk) == 0 and tq % min(sub, tq) == 0
    return pl.pallas_call(
        functools.partial(_mla_attn_kernel, tk=tk, sub=min(sub, tq)),
        out_shape=jax.ShapeDtypeStruct((m, MLA_HEADS * MLA_V), BF16),
        grid=(nseq, MLA_HEADS, qt),
        in_specs=[pl.BlockSpec((tq, MLA_HPAD), lambda b, h, i: (b * qt + i, h)),
                  pl.BlockSpec((seq, MLA_HPAD), lambda b, h, i: (b, h)),
                  pl.BlockSpec((seq, MLA_HPAD), lambda b, h, i: (b, h))],
        out_specs=pl.BlockSpec((tq, MLA_V), lambda b, h, i: (b * qt + i, h)),
        scratch_shapes=[pltpu.VMEM((tq, LANE), F32), pltpu.VMEM((tq, MLA_HPAD), F32),
                        pltpu.VMEM((2, tq, tk), F32)],
        compiler_params=_params(("parallel", "parallel", "arbitrary")),
        name="mla_attn",
    )(q, k, v)


def _merge_kernel(x_ref, wg0, wg1, wg2, ya, yb, yc, wb0, wb1, wb2, o_ref):
    x = x_ref[...]
    acc = None
    for wg, y, wb in ((wg0, ya, wb0), (wg1, yb, wb1), (wg2, yc, wb2)):
        gate = _sigmoid(jnp.dot(x, wg[...], preferred_element_type=F32))
        term = gate * jnp.dot(y[...], wb[...], preferred_element_type=F32)
        acc = term if acc is None else acc + term
    o_ref[...] = acc.astype(o_ref.dtype)


def _merge(x, w_gate, ys, w_branch, tm, tn):
    m, k = x.shape
    n = w_branch.shape[2]
    nj = n // tn
    kb = w_branch.shape[1]
    gate_specs = [pl.BlockSpec((k, tn), functools.partial(lambda i, j, b: (0, b * nj + j), b=b)) for b in range(3)]
    y_specs = [pl.BlockSpec((tm, kb), lambda i, j: (i, 0))] * 3
    br_specs = [pl.BlockSpec((None, kb, tn), functools.partial(lambda i, j, b: (b, 0, j), b=b)) for b in range(3)]
    return pl.pallas_call(
        _merge_kernel,
        out_shape=jax.ShapeDtypeStruct((m, n), BF16),
        grid=(m // tm, nj),
        in_specs=[pl.BlockSpec((tm, k), lambda i, j: (i, 0))] + gate_specs + y_specs + br_specs,
        out_specs=pl.BlockSpec((tm, tn), lambda i, j: (i, j)),
        compiler_params=_params(("parallel", "arbitrary")),
        name="merge",
    )(x, w_gate, w_gate, w_gate, *ys, w_branch, w_branch, w_branch)


def _prep_stacks(w_out, ffn_g, ffn_u, ffn_d):
    fpad = D_FF_PAD - D_FF
    return dict(
        w_out=w_out.astype(BF16),
        ffn_g=ffn_g,
        ffn_u=ffn_u,
        ffn_d=jnp.pad(ffn_d, ((0, 0), (0, 0), (0, fpad), (0, 0))).astype(BF16),
    )


def _prep_layer(w_in, gate_w, gate_b, norm_g, q_norm_g, uq, kv_norm_g, ukv, w_branch, ln_g, ln_b):
    d = D_MODEL
    n_a = 2 * GLA_HEADS * GLA_DK + 2 * GLA_HEADS * GLA_DV
    n_rank = 2 * GLA_RANK
    n_b = len(DIL_DILATIONS) * 3 * BRANCH_W
    n_c = MLA_QRANK + MLA_KVRANK + MLA_ROPE
    o_b = n_a + n_rank
    o_c = o_b + n_b
    o_g = o_c + n_c
    w_c = jnp.concatenate([w_in[:, o_c:o_g], jnp.zeros((d, LANE - MLA_ROPE), F32),
                           w_in[:, n_a:o_b], jnp.zeros((d, LANE - n_rank), F32)], axis=1)
    gw = jnp.zeros((LANE, 2 * GLA_HEADS * GLA_DK), F32)
    gw = gw.at[:GLA_RANK, :GLA_HEADS * GLA_DK].set(gate_w[0])
    gw = gw.at[GLA_RANK:2 * GLA_RANK, GLA_HEADS * GLA_DK:].set(gate_w[1])
    uq_pad = jnp.pad(uq.reshape(MLA_QRANK, MLA_HEADS, MLA_NOPE + MLA_ROPE),
                     ((0, 0), (0, 0), (0, MLA_HPAD - MLA_NOPE - MLA_ROPE))).reshape(MLA_QRANK, MLA_HEADS * MLA_HPAD)
    ukv3 = ukv.reshape(MLA_KVRANK, MLA_HEADS, MLA_NOPE + MLA_V)
    return dict(
        w_a=w_in[:, :n_a].astype(BF16),
        w_dil=[w_in[:, o_b + g * 3 * BRANCH_W:o_b + (g + 1) * 3 * BRANCH_W].astype(BF16)
               for g in range(len(DIL_DILATIONS))],
        w_c=w_c.astype(BF16),
        w_gate=w_in[:, o_g:].astype(BF16),
        gla_gw=gw.astype(BF16),
        gla_gb=jnp.concatenate([gate_b[0], gate_b[1]])[None, :],
        gla_norm=norm_g[None, :],
        q_norm=q_norm_g[None, :],
        kv_norm=kv_norm_g[None, :],
        w_uq=uq_pad.astype(BF16),
        w_kn=ukv3[:, :, :MLA_NOPE].reshape(MLA_KVRANK, -1).astype(BF16),
        w_vv=ukv3[:, :, MLA_NOPE:].reshape(MLA_KVRANK, -1).astype(BF16),
        w_branch=w_branch.astype(BF16),
        ln_g=ln_g[:, None, :],
        ln_b=ln_b[:, None, :],
    )


def _rope_tables(seq):
    pos = jnp.arange(seq, dtype=F32)[:, None]
    half = DIL_HD // 2
    ang = pos * jnp.power(ROPE_THETA, -2.0 * jnp.arange(half, dtype=F32) / DIL_HD)[None, :]
    cos128 = jnp.concatenate([jnp.cos(ang), jnp.cos(ang)], axis=1)
    sin128 = jnp.concatenate([-jnp.sin(ang), jnp.sin(ang)], axis=1)
    half = MLA_ROPE // 2
    ang = pos * jnp.power(ROPE_THETA, -2.0 * jnp.arange(half, dtype=F32) / MLA_ROPE)[None, :]
    zero = jnp.zeros((seq, half), F32)
    pad = jnp.zeros((seq, LANE - MLA_ROPE), F32)
    cos64 = jnp.concatenate([jnp.cos(ang), jnp.cos(ang), pad], axis=1)
    sin_a = jnp.concatenate([-jnp.sin(ang), zero, pad], axis=1)
    sin_b = jnp.concatenate([zero, jnp.sin(ang), pad], axis=1)
    return cos128, sin128, cos64, sin_a, sin_b


def _layer(x, xb, l, p, stk, tabs, seq):
    cos128, sin128, cos64, sin_a, sin_b = tabs
    h = _ffn_up(xb, stk["ffn_g"], stk["ffn_u"], (l, 0), D_FF_PAD, 1024, 512)
    x, xb = _mm_res_ln(h, stk["ffn_d"], (l, 0), x, p["ln_g"][0], p["ln_b"][0], 0.5, 512, 512, "ffn_down_ln")

    pa = _proj(xb, p["w_a"], F32, 1024, 512, "proj_gla")
    c = _proj(xb, p["w_c"], F32, 512, C_COLS, "proj_c")

    lg = _gla_gates(c, p["gla_gw"], p["gla_gb"], 512)
    y_a = _gla(pa, lg, p["gla_norm"], seq)

    outs, lses = [], []
    for g, d in enumerate(DIL_DILATIONS):
        qkv = _proj_dil(xb, p["w_dil"][g], cos128, sin128, seq, d, 1024, 1024, f"proj_dil_{g}")
        o, lse = _dilated_group(qkv, g, seq)
        outs.append(o)
        lses.append(lse)
    y_b = _dil_combine(outs, lses, 256)

    q_pad, k_pad, v_m = _mla_post(c, p["q_norm"], p["kv_norm"], p["w_uq"], p["w_kn"], p["w_vv"],
                                  cos64, sin_a, sin_b, seq, 256)
    y_c = _mla_attn(q_pad, k_pad, v_m, seq, 1024, 1024)

    merged = _merge(xb, p["w_gate"], (y_a, y_b, y_c), p["w_branch"], 512, 256)
    x, xb = _mm_res_ln(merged, stk["w_out"], (l,), x, p["ln_g"][1], p["ln_b"][1], 1.0, 512, 512, "out_proj_ln")

    h = _ffn_up(xb, stk["ffn_g"], stk["ffn_u"], (l, 1), D_FF_PAD, 1024, 512)
    return _mm_res_ln(h, stk["ffn_d"], (l, 1), x, p["ln_g"][2], p["ln_b"][2], 0.5, 512, 512, "ffn_down_ln")


def _trunk(x3, layers, stk):
    bsz, seq, d = x3.shape
    x = x3.reshape(bsz * seq, d)
    xb = x.astype(BF16)
    tabs = _rope_tables(seq)
    for l, p in enumerate(layers):
        x, xb = _layer(x, xb, l, p, stk, tabs, seq)
    return x.reshape(bsz, seq, d)


def kernel(x_prompt, x_sample, w_in, gla_gate_w, gla_gate_b, gla_norm_g, mla_q_norm_g, mla_uq, mla_kv_norm_g,
           mla_ukv, w_branch, w_out, ffn_w_gate, ffn_w_up, ffn_w_down, ln_g, ln_b):
    layers = [_prep_layer(w_in[l], gla_gate_w[l], gla_gate_b[l], gla_norm_g[l], mla_q_norm_g[l], mla_uq[l],
                          mla_kv_norm_g[l], mla_ukv[l], w_branch[l], ln_g[l], ln_b[l]) for l in range(N_LAYERS)]
    stk = _prep_stacks(w_out, ffn_w_gate, ffn_w_up, ffn_w_down)
    return (_trunk(x_prompt, layers, stk), _trunk(x_sample, layers, stk))
```
